```python
import math
import jax
import jax.numpy as jnp
from jax import lax
import numpy as np

D_MODEL = 4096
BATCH = 4
SEQ = 2048
DEPTH = 2

CTX_LEN = 256
GRID_W = 64
N_BRANCH = 4
BRANCH_W = 1024
RMS_EPS = 1e-6
N_MOD = 6
ROPE_BASE = 10000.0
S5_GROUP = 16
S5_GROUPS = BRANCH_W // S5_GROUP
S5_STATE = 64
S5_DT_MIN = 0.001
S5_DT_MAX = 0.1
DN_HEADS = 8
DN_HEAD_DIM = BRANCH_W // DN_HEADS
DN_CHUNK = 64
SHORT_CONV = 4
DA_HEADS = 8
DA_HEAD_DIM = BRANCH_W // (2 * DA_HEADS)
DA_V_DIM = 2 * DA_HEAD_DIM
Q_BLOCK = 128
LRU_BLOCKS = 8
LRU_BLOCK_W = BRANCH_W // LRU_BLOCKS
LRU_C = 8.0
LRU_CONV = 4
N_EXPERTS = 32
TOP_K = 4
EXPERT_FF = 512
SWIGLU_LIMIT = 7.0
SWIGLU_ALPHA = 1.702
PROJ_SIZES = (BRANCH_W, 3 * BRANCH_W, BRANCH_W, 2 * DN_HEADS, 2 * DN_HEADS, BRANCH_W, BRANCH_W, BRANCH_W, BRANCH_W, BRANCH_W, N_BRANCH * D_MODEL)
PROJ_COLS = 10 * BRANCH_W + 4 * DN_HEADS + N_BRANCH * D_MODEL

kernel_name = 'hybrid_flow_block_s5_deltanet_diffattn_rglru_moe'


def _rmsnorm(x, g):
    xf = x.astype(jnp.float32)
    y = xf * lax.rsqrt(jnp.mean(xf * xf, axis=-1, keepdims=True) + RMS_EPS)
    return (y * g.astype(jnp.float32)).astype(x.dtype)


def _modulate(h, shift, scale):
    return h * (1.0 + scale) + shift


def _split_proj(p):
    cuts = np.cumsum(PROJ_SIZES)[:-1].tolist()
    return jnp.split(p, cuts, axis=-1)


def _flip(t, rev):
    return jnp.flip(t, axis=1) if rev else t


def _linear_combine(e1, e2):
    a1, b1 = e1
    a2, b2 = e2
    return a1 * a2, a2 * b1 + b2


def _dwconv(x, w, b=None):
    k = w.shape[0]
    y = lax.conv_general_dilated(x, w[:, None, :].astype(x.dtype), (1,), [(k // 2, k - 1 - k // 2)], dimension_numbers=('NWC', 'WIO', 'NWC'), feature_group_count=x.shape[-1])
    return y if b is None else y + b.astype(x.dtype)


def _l2norm(t):
    return t * lax.rsqrt(jnp.sum(t * t, axis=-1, keepdims=True) + 1e-6)


def _axial_rope(n_tok, rot_dim):
    n_rows = n_tok // GRID_W
    rows = jnp.repeat(jnp.arange(n_rows), GRID_W).astype(jnp.float32)
    cols = jnp.tile(jnp.arange(GRID_W), n_rows).astype(jnp.float32)
    n_freq = rot_dim // 4
    inv = ROPE_BASE ** (-jnp.arange(n_freq, dtype=jnp.float32) / n_freq)
    ang = jnp.concatenate([rows[:, None] * inv, cols[:, None] * inv], axis=-1)
    return jnp.cos(ang), jnp.sin(ang)


def _rope(x, cos, sin):
    half = x.shape[-1] // 2
    shp = (1, cos.shape[0]) + (1,) * (x.ndim - 3) + (half,)
    c, s = cos.reshape(shp), sin.reshape(shp)
    x1, x2 = x[..., :half], x[..., half:]
    return jnp.concatenate([x1 * c - x2 * s, x1 * s + x2 * c], axis=-1).astype(x.dtype)


def _s5_scan(u, lam_bar, b_bar, c_mat, h0):
    bu = jnp.einsum('blgc,gpc->blgp', u.astype(jnp.complex64), b_bar)
    bu = bu.at[:, 0].add(lam_bar * h0)
    a = jnp.broadcast_to(lam_bar, bu.shape)
    _, h = lax.associative_scan(_linear_combine, (a, bu), axis=1)
    y = jnp.einsum('blgp,gcp->blgc', h, c_mat).real
    return y, h[:, -1]


def _s5_mixer(u_ctx, u_lat, p):
    def groups(u):
        return u.astype(jnp.float32).reshape(u.shape[0], u.shape[1], S5_GROUPS, S5_GROUP)
    uc, ul = groups(u_ctx), groups(u_lat)
    y_ctx = jnp.zeros_like(uc)
    y_lat = jnp.zeros_like(ul)
    for d, rev in enumerate((False, True)):
        lam = lax.complex(p['s5_lam_re'][d].astype(jnp.float32), p['s5_lam_im'][d].astype(jnp.float32))
        dt = jnp.exp(p['s5_log_dt'][d].astype(jnp.float32))[:, None]
        lam_bar = jnp.exp(lam * dt)
        b_mat = lax.complex(p['s5_b_re'][d].astype(jnp.float32), p['s5_b_im'][d].astype(jnp.float32))
        b_bar = ((lam_bar - 1.0) / lam)[..., None] * b_mat
        c_mat = lax.complex(p['s5_c_re'][d].astype(jnp.float32), p['s5_c_im'][d].astype(jnp.float32))
        h0 = jnp.zeros((uc.shape[0], S5_GROUPS, S5_STATE), jnp.complex64)
        yc, h_ctx = _s5_scan(_flip(uc, rev), lam_bar, b_bar, c_mat, h0)
        yl, _ = _s5_scan(_flip(ul, rev), lam_bar, b_bar, c_mat, h_ctx)
        y_ctx = y_ctx + _flip(yc, rev)
        y_lat = y_lat + _flip(yl, rev)
    d_skip = p['s5_d'].astype(jnp.float32).reshape(S5_GROUPS, S5_GROUP)
    w_glu = p['s5_w_glu'].astype(jnp.float32)

    def readout(y, u, like):
        z = jax.nn.gelu((y + d_skip * u).reshape(u.shape[0], u.shape[1], BRANCH_W))
        return (z * jax.nn.sigmoid(z @ w_glu)).astype(like.dtype)
    return readout(y_ctx, uc, u_ctx), readout(y_lat, ul, u_lat)


def _gated_delta_rule(q, k, v, g, beta, s0):
    bsz, n_tok, n_h, _ = q.shape
    dv = v.shape[-1]
    n_chunk = n_tok // DN_CHUNK

    def chunks(t):
        return jnp.moveaxis(t.reshape(bsz, n_chunk, DN_CHUNK, n_h, -1), (1, 3), (0, 2))
    qc, kc, vc = chunks(q), chunks(k), chunks(v)
    gcum = jnp.cumsum(chunks(g[..., None])[..., 0], axis=-1)
    bc = chunks(beta[..., None])[..., 0]
    causal = jnp.tril(jnp.ones((DN_CHUNK, DN_CHUNK), bool))
    strict = jnp.tril(jnp.ones((DN_CHUNK, DN_CHUNK), bool), -1)
    decay = jnp.exp(jnp.where(causal, gcum[..., :, None] - gcum[..., None, :], -jnp.inf))
    kb = kc * bc[..., None]
    a_low = jnp.where(strict, jnp.einsum('nbhid,nbhjd->nbhij', kb, kc) * decay, 0.0)
    eye = jnp.eye(DN_CHUNK, dtype=jnp.float32)
    t_inv = lax.linalg.triangular_solve(a_low + eye, jnp.broadcast_to(eye, a_low.shape), left_side=True, lower=True, unit_diagonal=True)
    u = t_inv @ (vc * bc[..., None])
    w = t_inv @ (kb * jnp.exp(gcum)[..., None])
    qk = jnp.einsum('nbhid,nbhjd->nbhij', qc, kc) * decay

    def step(s, inp):
        q_i, k_i, u_i, w_i, g_i, qk_i = inp
        v_new = u_i - w_i @ s
        o = (q_i * jnp.exp(g_i)[..., None]) @ s + qk_i @ v_new
        g_last = g_i[..., -1:]
        k_dec = k_i * jnp.exp(g_last - g_i)[..., None]
        s = s * jnp.exp(g_last)[..., None] + jnp.einsum('bhcd,bhce->bhde', k_dec, v_new)
        return s, o
    s_fin, o = lax.scan(step, s0, (qc, kc, u, w, gcum, qk))
    o = jnp.moveaxis(o, (0, 2), (1, 3)).reshape(bsz, n_tok, n_h, dv)
    return o, s_fin


def _dn_prep(qkv, a, b, p):
    bsz, n = qkv.shape[:2]
    qkv = jax.nn.silu(_dwconv(qkv, p['dn_conv_w'])).astype(jnp.float32).reshape(bsz, n, 3, DN_HEADS, DN_HEAD_DIM)
    q = _l2norm(qkv[:, :, 0]) * DN_HEAD_DIM ** -0.5
    k = _l2norm(qkv[:, :, 1])
    v = qkv[:, :, 2]
    a = a.astype(jnp.float32).reshape(bsz, n, 2, DN_HEADS)
    g = -jnp.exp(p['dn_a_log'].astype(jnp.float32)) * jax.nn.softplus(a + p['dn_dt_bias'].astype(jnp.float32))
    beta = jax.nn.sigmoid(b.astype(jnp.float32).reshape(bsz, n, 2, DN_HEADS))
    return q, k, v, g, beta


def _deltanet_mixer(qkv_c, a_c, b_c, z_c, qkv_l, a_l, b_l, z_l, p):
    ctx_in = _dn_prep(qkv_c, a_c, b_c, p)
    lat_in = _dn_prep(qkv_l, a_l, b_l, p)
    bsz = qkv_l.shape[0]
    o_ctx = jnp.zeros_like(ctx_in[2])
    o_lat = jnp.zeros_like(lat_in[2])
    for d, rev in enumerate((False, True)):
        def dir_inputs(t):
            q, k, v, g, beta = t
            return [_flip(z, rev) for z in (q, k, v, g[:, :, d], beta[:, :, d])]
        s0 = jnp.zeros((bsz, DN_HEADS, DN_HEAD_DIM, DN_HEAD_DIM), jnp.float32)
        oc, s_ctx = _gated_delta_rule(*dir_inputs(ctx_in), s0)
        ol, _ = _gated_delta_rule(*dir_inputs(lat_in), s_ctx)
        o_ctx = o_ctx + _flip(oc, rev)
        o_lat = o_lat + _flip(ol, rev)

    def readout(o, z):
        zz = z.astype(jnp.float32).reshape(o.shape)
        y = _rmsnorm(o, p['dn_norm']) * jax.nn.silu(zz)
        return y.reshape(o.shape[0], o.shape[1], BRANCH_W).astype(z.dtype)
    return readout(o_ctx, z_c), readout(o_lat, z_l)


def _diff_attn_mixer(q_c, k_c, v_c, q_l, k_l, v_l, p, lam_init):
    bsz, n_lat = q_l.shape[:2]

    def qk_heads(t):
        return t.astype(jnp.float32).reshape(t.shape[0], t.shape[1], DA_HEADS, 2, DA_HEAD_DIM)

    def v_heads(t):
        return t.astype(jnp.float32).reshape(t.shape[0], t.shape[1], DA_HEADS, DA_V_DIM)
    cos, sin = _axial_rope(n_lat, DA_HEAD_DIM)
    qc, kc, vc = qk_heads(q_c), qk_heads(k_c), v_heads(v_c)
    ql, kl, vl = _rope(qk_heads(q_l), cos, sin), _rope(qk_heads(k_l), cos, sin), v_heads(v_l)
    f32 = lambda t: t.astype(jnp.float32)
    lam = (jnp.exp(jnp.sum(f32(p['da_lam_q1']) * f32(p['da_lam_k1']))) - jnp.exp(jnp.sum(f32(p['da_lam_q2']) * f32(p['da_lam_k2']))) + lam_init)
    scale = DA_HEAD_DIM ** -0.5

    def attend(q, k, v):
        s = jnp.einsum('bqhmd,bkhmd->bhmqk', q, k) * scale
        pr = jax.nn.softmax(s, axis=-1)
        w = pr[:, :, 0] - lam * pr[:, :, 1]
        return jnp.einsum('bhqk,bkhe->bqhe', w, v)
    o_ctx = attend(qc, kc, vc)
    k_all = jnp.concatenate([kl, kc], axis=1)
    v_all = jnp.concatenate([vl, vc], axis=1)
    n_blk = n_lat // Q_BLOCK
    q_blocks = jnp.moveaxis(ql.reshape(bsz, n_blk, Q_BLOCK, DA_HEADS, 2, DA_HEAD_DIM), 1, 0)
    o_lat = lax.map(lambda qb: attend(qb, k_all, v_all), q_blocks)
    o_lat = jnp.moveaxis(o_lat, 0, 1).reshape(bsz, n_lat, DA_HEADS, DA_V_DIM)

    def readout(o, like):
        y = _rmsnorm(o, p['da_subln']) * (1.0 - lam_init)
        return y.reshape(o.shape[0], o.shape[1], BRANCH_W).astype(like.dtype)
    return readout(o_ctx, q_c), readout(o_lat, q_l)


def _rglru_scan(x, w_r, b_r, w_i, b_i, lam, h0):
    bsz, n, _ = x.shape
    xb = x.reshape(bsz, n, LRU_BLOCKS, LRU_BLOCK_W)
    r = jax.nn.sigmoid(jnp.einsum('blhi,hij->blhj', xb, w_r).reshape(bsz, n, BRANCH_W) + b_r)
    i = jax.nn.sigmoid(jnp.einsum('blhi,hij->blhj', xb, w_i).reshape(bsz, n, BRANCH_W) + b_i)
    log_a = -LRU_C * r * jax.nn.softplus(-lam)
    a = jnp.exp(log_a)
    b = jnp.sqrt(-jnp.expm1(2.0 * log_a)) * (i * x)
    b = b.at[:, 0].add(a[:, 0] * h0)
    _, h = lax.associative_scan(_linear_combine, (a, b), axis=1)
    return h, h[:, -1]


def _rglru_mixer(x_c, g_c, x_l, g_l, p):
    xc = _dwconv(x_c, p['lru_conv_w'], p['lru_conv_b']).astype(jnp.float32)
    xl = _dwconv(x_l, p['lru_conv_w'], p['lru_conv_b']).astype(jnp.float32)
    h_ctx = jnp.zeros_like(xc)
    h_lat = jnp.zeros_like(xl)
    for d, rev in enumerate((False, True)):
        prm = tuple(p[name][d].astype(jnp.float32) for name in ('lru_w_rg', 'lru_b_rg', 'lru_w_ig', 'lru_b_ig', 'lru_lam'))
        h0 = jnp.zeros((xc.shape[0], BRANCH_W), jnp.float32)
        hc, h_fin = _rglru_scan(_flip(xc, rev), *prm, h0)
        hl, _ = _rglru_scan(_flip(xl, rev), *prm, h_fin)
        h_ctx = h_ctx + _flip(hc, rev)
        h_lat = h_lat + _flip(hl, rev)
    return ((h_ctx * jax.nn.gelu(g_c.astype(jnp.float32))).astype(g_c.dtype), (h_lat * jax.nn.gelu(g_l.astype(jnp.float32))).astype(g_l.dtype))


def _merge(branches, gate_logits, w_branch, w_out):
    bsz, n, _ = gate_logits.shape
    gates = jax.nn.sigmoid(gate_logits.reshape(bsz, n, N_BRANCH, D_MODEL))
    merged = gates[:, :, 0] * (branches[0] @ w_branch[0])
    for k in range(1, N_BRANCH):
        merged = merged + gates[:, :, k] * (branches[k] @ w_branch[k])
    return merged @ w_out


def _token_mixer(h_lat, h_ctx, p, lam_init, with_ctx):
    (u_l, qkv_l, z_l, a_l, b_l, q_l, k_l, v_l, x_l, g_l, m_l) = _split_proj(h_lat @ p['w_in'])
    (u_c, qkv_c, z_c, a_c, b_c, q_c, k_c, v_c, x_c, g_c, m_c) = _split_proj(h_ctx @ p['w_in'])
    s5_c, s5_l = _s5_mixer(u_c, u_l, p)
    dn_c, dn_l = _deltanet_mixer(qkv_c, a_c, b_c, z_c, qkv_l, a_l, b_l, z_l, p)
    da_c, da_l = _diff_attn_mixer(q_c, k_c, v_c, q_l, k_l, v_l, p, lam_init)
    lru_c, lru_l = _rglru_mixer(x_c, g_c, x_l, g_l, p)
    y_lat = _merge((s5_l, dn_l, da_l, lru_l), m_l, p['w_branch'], p['w_out'])
    if not with_ctx:
        return y_lat, None
    y_ctx = _merge((s5_c, dn_c, da_c, lru_c), m_c, p['w_branch'], p['w_out'])
    return y_lat, y_ctx


def _moe(h, p):
    shp = h.shape
    t = h.reshape(-1, shp[-1])
    logits = (t @ p['router_w'] + p['router_b']).astype(jnp.float32)
    top_v, top_i = lax.top_k(logits, TOP_K)
    wts = jax.nn.softmax(top_v, axis=-1)
    comb = jnp.sum(jax.nn.one_hot(top_i, N_EXPERTS, dtype=jnp.float32) * wts[..., None], axis=1).astype(t.dtype)
    gu = jnp.einsum('td,edf->tef', t, p['moe_w_gu']) + p['moe_b_gu']
    gate = jnp.minimum(gu[..., 0::2], SWIGLU_LIMIT)
    up = jnp.clip(gu[..., 1::2], -SWIGLU_LIMIT, SWIGLU_LIMIT)
    act = (up + 1.0) * (gate * jax.nn.sigmoid(SWIGLU_ALPHA * gate)) * comb[..., None]
    out = jnp.einsum('tef,efd->td', act, p['moe_w_down']) + comb @ p['moe_b_down']
    return out.reshape(shp)


def setup_inputs(seed: int = 0) -> dict:
    key = jax.random.key(seed)
    ks = iter(jax.random.split(key, 64))
    L, D, W, G, P, H = DEPTH, D_MODEL, BRANCH_W, S5_GROUPS, S5_STATE, DN_HEADS

    def nrm(shape, scale):
        return scale * jax.random.normal(next(ks), shape, jnp.float32)

    def gain(shape):
        return 1.0 + nrm(shape, 0.05)

    def unif(shape, lo, hi):
        return jax.random.uniform(next(ks), shape, jnp.float32, lo, hi)
    x = nrm((BATCH, SEQ, D), 1.0)
    c = nrm((BATCH, D), 1.0)
    ctx = nrm((BATCH, CTX_LEN, D), 1.0)
    c_ctx = nrm((D,), 1.0)
    ada_w = nrm((L, D, N_MOD * D), 0.5 * D ** -0.5)
    ada_b = nrm((L, N_MOD * D), 0.02)
    mix_norm = gain((L, D))
    ffn_norm = gain((L, D))
    w_in = nrm((L, D, PROJ_COLS), D ** -0.5)
    s5_lam_re = -0.5 * jnp.exp(nrm((L, 2, G, P), 0.02))
    s5_lam_im = jnp.pi * jnp.arange(P, dtype=jnp.float32) * (1.0 + nrm((L, 2, G, P), 0.01))
    s5_log_dt = unif((L, 2, G), math.log(S5_DT_MIN), math.log(S5_DT_MAX))
    s5_b_re = nrm((L, 2, G, P, S5_GROUP), (2 * S5_GROUP) ** -0.5)
    s5_b_im = nrm((L, 2, G, P, S5_GROUP), (2 * S5_GROUP) ** -0.5)
    s5_c_re = nrm((L, 2, G, S5_GROUP, P), P ** -0.5)
    s5_c_im = nrm((L, 2, G, S5_GROUP, P), P ** -0.5)
    s5_d = nrm((L, W), 1.0)
    s5_w_glu = nrm((L, W, W), W ** -0.5)
    dn_conv_w = nrm((L, SHORT_CONV, 3 * W), SHORT_CONV ** -0.5)
    dn_a_log = jnp.log(unif((L, 2, H), 1.0, 16.0))
    dn_dt = jnp.exp(unif((L, 2, H), math.log(0.001), math.log(0.1)))
    dn_dt_bias = dn_dt + jnp.log(-jnp.expm1(-dn_dt))
    dn_norm = gain((L, DN_HEAD_DIM))
    da_lam_q1 = nrm((L, DA_HEAD_DIM), 0.1)
    da_lam_k1 = nrm((L, DA_HEAD_DIM), 0.1)
    da_lam_q2 = nrm((L, DA_HEAD_DIM), 0.1)
    da_lam_k2 = nrm((L, DA_HEAD_DIM), 0.1)
    da_subln = gain((L, DA_V_DIM))
    lru_conv_w = nrm((L, LRU_CONV, W), LRU_CONV ** -0.5)
    lru_conv_b = nrm((L, W), 0.02)
    lru_w_rg = nrm((L, 2, LRU_BLOCKS, LRU_BLOCK_W, LRU_BLOCK_W), LRU_BLOCK_W ** -0.5)
    lru_b_rg = nrm((L, 2, W), 0.02)
    lru_w_ig = nrm((L, 2, LRU_BLOCKS, LRU_BLOCK_W, LRU_BLOCK_W), LRU_BLOCK_W ** -0.5)
    lru_b_ig = nrm((L, 2, W), 0.02)
    a_base = unif((L, 2, W), 0.9, 0.999) ** (1.0 / LRU_C)
    lru_lam = jnp.log(a_base) - jnp.log1p(-a_base)
    w_branch = nrm((L, N_BRANCH, W, D), W ** -0.5)
    w_out = nrm((L, D, D), D ** -0.5)
    router_w = nrm((L, D, N_EXPERTS), D ** -0.5)
    router_b = nrm((L, N_EXPERTS), 0.01)
    moe_w_gu = nrm((L, N_EXPERTS, D, 2 * EXPERT_FF), D ** -0.5)
    moe_b_gu = nrm((L, N_EXPERTS, 2 * EXPERT_FF), 0.02)
    moe_w_down = nrm((L, N_EXPERTS, EXPERT_FF, D), EXPERT_FF ** -0.5)
    moe_b_down = nrm((L, N_EXPERTS, D), 0.02)
    final_norm = gain((D,))
    return {'x': x, 'c': c, 'ctx': ctx, 'c_ctx': c_ctx, 'ada_w': ada_w, 'ada_b': ada_b, 'mix_norm': mix_norm, 'ffn_norm': ffn_norm, 'w_in': w_in, 's5_lam_re': s5_lam_re, 's5_lam_im': s5_lam_im, 's5_log_dt': s5_log_dt, 's5_b_re': s5_b_re, 's5_b_im': s5_b_im, 's5_c_re': s5_c_re, 's5_c_im': s5_c_im, 's5_d': s5_d, 's5_w_glu': s5_w_glu, 'dn_conv_w': dn_conv_w, 'dn_a_log': dn_a_log, 'dn_dt_bias': dn_dt_bias, 'dn_norm': dn_norm, 'da_lam_q1': da_lam_q1, 'da_lam_k1': da_lam_k1, 'da_lam_q2': da_lam_q2, 'da_lam_k2': da_lam_k2, 'da_subln': da_subln, 'lru_conv_w': lru_conv_w, 'lru_conv_b': lru_conv_b, 'lru_w_rg': lru_w_rg, 'lru_b_rg': lru_b_rg, 'lru_w_ig': lru_w_ig, 'lru_b_ig': lru_b_ig, 'lru_lam': lru_lam, 'w_branch': w_branch, 'w_out': w_out, 'router_w': router_w, 'router_b': router_b, 'moe_w_gu': moe_w_gu, 'moe_b_gu': moe_b_gu, 'moe_w_down': moe_w_down, 'moe_b_down': moe_b_down, 'final_norm': final_norm}


def reference(x, c, ctx, c_ctx, ada_w, ada_b, mix_norm, ffn_norm, w_in, s5_lam_re, s5_lam_im, s5_log_dt, s5_b_re, s5_b_im, s5_c_re, s5_c_im, s5_d, s5_w_glu, dn_conv_w, dn_a_log, dn_dt_bias, dn_norm, da_lam_q1, da_lam_k1, da_lam_q2, da_lam_k2, da_subln, lru_conv_w, lru_conv_b, lru_w_rg, lru_b_rg, lru_w_ig, lru_b_ig, lru_lam, w_branch, w_out, router_w, router_b, moe_w_gu, moe_b_gu, moe_w_down, moe_b_down, final_norm):
    bsz = x.shape[0]
    x_lat, x_ctx = x, ctx
    for l in range(DEPTH):
        with_ctx = l < DEPTH - 1
        lam_init = 0.8 - 0.6 * math.exp(-0.3 * l)
        p = {'w_in': w_in[l], 's5_lam_re': s5_lam_re[l], 's5_lam_im': s5_lam_im[l], 's5_log_dt': s5_log_dt[l], 's5_b_re': s5_b_re[l], 's5_b_im': s5_b_im[l], 's5_c_re': s5_c_re[l], 's5_c_im': s5_c_im[l], 's5_d': s5_d[l], 's5_w_glu': s5_w_glu[l], 'dn_conv_w': dn_conv_w[l], 'dn_a_log': dn_a_log[l], 'dn_dt_bias': dn_dt_bias[l], 'dn_norm': dn_norm[l], 'da_lam_q1': da_lam_q1[l], 'da_lam_k1': da_lam_k1[l], 'da_lam_q2': da_lam_q2[l], 'da_lam_k2': da_lam_k2[l], 'da_subln': da_subln[l], 'lru_conv_w': lru_conv_w[l], 'lru_conv_b': lru_conv_b[l], 'lru_w_rg': lru_w_rg[l], 'lru_b_rg': lru_b_rg[l], 'lru_w_ig': lru_w_ig[l], 'lru_b_ig': lru_b_ig[l], 'lru_lam': lru_lam[l], 'w_branch': w_branch[l], 'w_out': w_out[l], 'router_w': router_w[l], 'router_b': router_b[l], 'moe_w_gu': moe_w_gu[l], 'moe_b_gu': moe_b_gu[l], 'moe_w_down': moe_w_down[l], 'moe_b_down': moe_b_down[l]}
        mod_lat = jnp.moveaxis((jax.nn.silu(c) @ ada_w[l] + ada_b[l]).reshape(bsz, N_MOD, D_MODEL), 1, 0)[:, :, None, :]
        mod_ctx = (jax.nn.silu(c_ctx) @ ada_w[l] + ada_b[l]).reshape(N_MOD, D_MODEL)
        h_lat = _modulate(_rmsnorm(x_lat, mix_norm[l]), mod_lat[0], mod_lat[1])
        h_ctx = _modulate(_rmsnorm(x_ctx, mix_norm[l]), mod_ctx[0], mod_ctx[1])
        y_lat, y_ctx = _token_mixer(h_lat, h_ctx, p, lam_init, with_ctx)
        x_lat = x_lat + mod_lat[2] * y_lat
        h_lat = _modulate(_rmsnorm(x_lat, ffn_norm[l]), mod_lat[3], mod_lat[4])
        x_lat = x_lat + mod_lat[5] * _moe(h_lat, p)
        if with_ctx:
            x_ctx = x_ctx + mod_ctx[2] * y_ctx
            h_ctx = _modulate(_rmsnorm(x_ctx, ffn_norm[l]), mod_ctx[3], mod_ctx[4])
            x_ctx = x_ctx + mod_ctx[5] * _moe(h_ctx, p)
    return _rmsnorm(x_lat, final_norm)
```

```python
import functools
import math

import jax
import jax.numpy as jnp
import numpy as np
from jax import lax
from jax.experimental import pallas as pl
from jax.experimental.pallas import tpu as pltpu

F32 = jnp.float32
BF16 = jnp.bfloat16

GRID_W = 64
N_BRANCH = 4
BRANCH_W = 1024
RMS_EPS = 1e-6
N_MOD = 6
ROPE_BASE = 10000.0
S5_GROUP = 16
S5_GROUPS = BRANCH_W // S5_GROUP
S5_STATE = 64
DN_HEADS = 8
DN_HEAD_DIM = BRANCH_W // DN_HEADS
DN_CHUNK = 64
DA_HEADS = 8
DA_HEAD_DIM = BRANCH_W // (2 * DA_HEADS)
DA_V_DIM = 2 * DA_HEAD_DIM
Q_BLOCK = 128
LRU_BLOCKS = 8
LRU_BLOCK_W = BRANCH_W // LRU_BLOCKS
LRU_C = 8.0
TOP_K = 4
SWIGLU_LIMIT = 7.0
SWIGLU_ALPHA = 1.702

MOD_ROWS = 8
LANES = 128
MOE_TILE = 256
AB_COLS = 128
VMEM_LIMIT = 56 << 20


def _params(sem, vmem=VMEM_LIMIT):
    return pltpu.CompilerParams(dimension_semantics=sem, vmem_limit_bytes=vmem)


def _tile(n, pref):
    t = min(n, pref)
    while n % t:
        t //= 2
    return t


def _ada_kernel(c_ref, w_ref, b_ref, o_ref):
    @pl.when(pl.program_id(2) == 0)
    def _():
        o_ref[0] = jnp.broadcast_to(b_ref[0], o_ref.shape[1:])
    c = c_ref[...]
    s = (c * jax.nn.sigmoid(c)).astype(BF16)
    o_ref[0] += jnp.dot(s, w_ref[0].astype(BF16), preferred_element_type=F32)


def _ada_table(c, c_ctx, ada_w, ada_b):
    depth, d, nm = ada_w.shape
    bsz = c.shape[0]
    cs = jnp.zeros((MOD_ROWS, d), F32).at[:bsz].set(c).at[bsz].set(c_ctx)
    tk, tn = _tile(d, 1024), _tile(nm, 2048)
    mod = pl.pallas_call(
        _ada_kernel,
        grid=(depth, nm // tn, d // tk),
        in_specs=[pl.BlockSpec((MOD_ROWS, tk), lambda l, n, k: (0, k)),
                  pl.BlockSpec((1, tk, tn), lambda l, n, k: (l, k, n)),
                  pl.BlockSpec((1, 1, tn), lambda l, n, k: (l, 0, n))],
        out_specs=pl.BlockSpec((1, MOD_ROWS, tn), lambda l, n, k: (l, 0, n)),
        out_shape=jax.ShapeDtypeStruct((depth, MOD_ROWS, nm), F32),
        compiler_params=_params(("parallel", "parallel", "arbitrary")),
        name="ada_table",
    )(cs, ada_w, ada_b.reshape(depth, 1, nm))
    return jnp.moveaxis(mod.reshape(depth, MOD_ROWS, N_MOD, d), 1, 2).reshape(depth, N_MOD * MOD_ROWS, 1, d)


def _mod_index(which, tm, n_ctx, seq, bsz):
    def idx(i):
        r = i * tm
        return which * MOD_ROWS + jnp.where(r < n_ctx, bsz, (r - n_ctx) // seq)
    return idx


def _norm_mod_kernel(x_ref, g_ref, sh_ref, sc_ref, o_ref):
    x = x_ref[...]
    y = x * lax.rsqrt(jnp.mean(x * x, axis=-1, keepdims=True) + RMS_EPS) * g_ref[...]
    o_ref[...] = (y * (1.0 + sc_ref[0]) + sh_ref[0]).astype(o_ref.dtype)


def _norm_mod(x, gain, mod, geo):
    n, d = x.shape
    n_ctx, seq, bsz = geo
    tm = _tile(math.gcd(n_ctx, seq), 256)
    sh, sc = _mod_index(0, tm, n_ctx, seq, bsz), _mod_index(1, tm, n_ctx, seq, bsz)
    return pl.pallas_call(
        _norm_mod_kernel,
        grid=(n // tm,),
        in_specs=[pl.BlockSpec((tm, d), lambda i: (i, 0)),
                  pl.BlockSpec((1, d), lambda i: (0, 0)),
                  pl.BlockSpec((1, 1, d), lambda i: (sh(i), 0, 0)),
                  pl.BlockSpec((1, 1, d), lambda i: (sc(i), 0, 0))],
        out_specs=pl.BlockSpec((tm, d), lambda i: (i, 0)),
        out_shape=jax.ShapeDtypeStruct((n, d), BF16),
        compiler_params=_params(("parallel",)),
        name="norm_mod",
    )(x, gain.reshape(1, d), mod, mod)


def _mm_kernel(a_ref, b_ref, o_ref):
    o_ref[...] = jnp.dot(a_ref[...], b_ref[...], preferred_element_type=F32).astype(o_ref.dtype)


def _matmul(a, b, out_dtype, tm_pref=512, tn_pref=1152):
    m, k = a.shape
    n = b.shape[1]
    tm = _tile(m, tm_pref)
    tn = tn_pref if n % tn_pref == 0 else _tile(n, 1024)
    return pl.pallas_call(
        _mm_kernel,
        grid=(m // tm, n // tn),
        in_specs=[pl.BlockSpec((tm, k), lambda i, j: (i, 0)),
                  pl.BlockSpec((k, tn), lambda i, j: (0, j))],
        out_specs=pl.BlockSpec((tm, tn), lambda i, j: (i, j)),
        out_shape=jax.ShapeDtypeStruct((m, n), out_dtype),
        compiler_params=_params(("parallel", "parallel")),
        name="in_proj",
    )(a, b)


def _merge_kernel(h_ref, b0, b1, b2, b3, g0, g1, g2, g3, wb_ref, o_ref):
    h = h_ref[...]
    acc = None
    for k, (br, wg) in enumerate(((b0, g0), (b1, g1), (b2, g2), (b3, g3))):
        gate = jax.nn.sigmoid(jnp.dot(h, wg[...], preferred_element_type=F32))
        lifted = jnp.dot(br[...], wb_ref[k], preferred_element_type=F32)
        acc = gate * lifted if acc is None else acc + gate * lifted
    o_ref[...] = acc.astype(o_ref.dtype)


def _merge(h, branches, w_gate, w_branch, row0, rows):
    d = h.shape[1]
    w = branches[0].shape[1]
    tm, tn = _tile(math.gcd(row0, rows) if row0 else rows, 512), _tile(d, 256)
    off, nt = row0 // tm, d // tn
    gate_specs = [pl.BlockSpec((d, tn), functools.partial(lambda i, j, k: (0, k * nt + j), k=k)) for k in range(N_BRANCH)]
    return pl.pallas_call(
        _merge_kernel,
        grid=(rows // tm, nt),
        in_specs=[pl.BlockSpec((tm, d), lambda i, j: (i + off, 0))]
        + [pl.BlockSpec((tm, w), lambda i, j: (i + off, 0))] * N_BRANCH
        + gate_specs
        + [pl.BlockSpec((N_BRANCH, w, tn), lambda i, j: (0, 0, j))],
        out_specs=pl.BlockSpec((tm, tn), lambda i, j: (i, j)),
        out_shape=jax.ShapeDtypeStruct((rows, d), BF16),
        compiler_params=_params(("parallel", "parallel")),
        name="merge",
    )(h, *branches, w_gate, w_gate, w_gate, w_gate, w_branch)


def _outproj_kernel(m_ref, w_ref, x_ref, g_ref, o_ref):
    o_ref[...] = x_ref[...] + g_ref[0] * jnp.dot(m_ref[...], w_ref[...], preferred_element_type=F32)


def _outproj_residual(merged, w_out, x, mod, geo, row0, rows):
    n, d = x.shape
    n_ctx, seq, bsz = geo
    tm, tn = _tile(math.gcd(n_ctx, seq), 512), _tile(d, 1024)
    off = row0 // tm
    gi = _mod_index(2, tm, n_ctx, seq, bsz)
    return pl.pallas_call(
        _outproj_kernel,
        grid=(rows // tm, d // tn),
        in_specs=[pl.BlockSpec((tm, d), lambda i, j: (i, 0)),
                  pl.BlockSpec((d, tn), lambda i, j: (0, j)),
                  pl.BlockSpec((tm, tn), lambda i, j: (i + off, j)),
                  pl.BlockSpec((1, 1, tn), lambda i, j: (gi(i + off), 0, j))],
        out_specs=pl.BlockSpec((tm, tn), lambda i, j: (i, j)),
        out_shape=jax.ShapeDtypeStruct((rows, d), F32),
        compiler_params=_params(("parallel", "parallel")),
        name="out_proj",
    )(merged, w_out, x, mod)


def _router_kernel(n_experts, x_ref, g_ref, sh_ref, sc_ref, rw_ref, rb_ref, h_ref, ti_ref, tw_ref):
    x = x_ref[...]
    y = x * lax.rsqrt(jnp.mean(x * x, axis=-1, keepdims=True) + RMS_EPS) * g_ref[...]
    h = y * (1.0 + sc_ref[0]) + sh_ref[0]
    h_ref[...] = h
    logits = jnp.dot(h.astype(BF16), rw_ref[...], preferred_element_type=F32) + rb_ref[...]
    lane = lax.broadcasted_iota(jnp.int32, logits.shape, 1)
    logits = jnp.where(lane < n_experts, logits, -jnp.inf)
    idx_out = jnp.zeros(logits.shape, jnp.int32)
    val_out = jnp.zeros(logits.shape, F32)
    top = None
    denom = None
    for k in range(TOP_K):
        m = jnp.max(logits, axis=-1, keepdims=True)
        idx = jnp.min(jnp.where(logits == m, lane, LANES), axis=-1, keepdims=True)
        top = m if top is None else top
        e = jnp.exp(m - top)
        denom = e if denom is None else denom + e
        idx_out = jnp.where(lane == k, idx, idx_out)
        val_out = jnp.where(lane == k, e, val_out)
        logits = jnp.where(lane == idx, -jnp.inf, logits)
    ti_ref[...] = idx_out
    tw_ref[...] = val_out / denom


def _router(x, gain, mod, router_w, router_b, geo, row0, rows):
    n, d = x.shape
    n_ctx, seq, bsz = geo
    n_experts = router_w.shape[1]
    tm = _tile(math.gcd(n_ctx, seq), 256)
    off = row0 // tm
    sh, sc = _mod_index(3, tm, n_ctx, seq, bsz), _mod_index(4, tm, n_ctx, seq, bsz)
    rw = jnp.zeros((d, LANES), BF16).at[:, :n_experts].set(router_w.astype(BF16))
    rb = jnp.zeros((1, LANES), F32).at[0, :n_experts].set(router_b)
    return pl.pallas_call(
        functools.partial(_router_kernel, n_experts),
        grid=(rows // tm,),
        in_specs=[pl.BlockSpec((tm, d), lambda i: (i, 0)),
                  pl.BlockSpec((1, d), lambda i: (0, 0)),
                  pl.BlockSpec((1, 1, d), lambda i: (sh(i + off), 0, 0)),
                  pl.BlockSpec((1, 1, d), lambda i: (sc(i + off), 0, 0)),
                  pl.BlockSpec((d, LANES), lambda i: (0, 0)),
                  pl.BlockSpec((1, LANES), lambda i: (0, 0))],
        out_specs=[pl.BlockSpec((tm, d), lambda i: (i, 0)),
                   pl.BlockSpec((tm, LANES), lambda i: (i, 0)),
                   pl.BlockSpec((tm, LANES), lambda i: (i, 0))],
        out_shape=[jax.ShapeDtypeStruct((rows, d), F32),
                   jax.ShapeDtypeStruct((rows, LANES), jnp.int32),
                   jax.ShapeDtypeStruct((rows, LANES), F32)],
        compiler_params=_params(("parallel",)),
        name="router",
    )(x, gain.reshape(1, d), mod, mod, rw, rb)


def _row_copy(src_hbm, row, dst, r, sem):
    return pltpu.make_async_copy(src_hbm.at[pl.ds(row, 1)], dst.at[pl.ds(r, 1)], sem)


def _expert_kernel(te_ref, used_ref, src_ref, h_hbm, wg_ref, wu_ref, bg_ref, bu_ref, wd_ref, bd_ref, o_ref, xbuf, sem):
    g = pl.program_id(0)
    n_used = used_ref[0]
    slot = g % 2

    def issue(tile, s):
        def body(r, carry):
            _row_copy(h_hbm, src_ref[tile * MOE_TILE + r], xbuf.at[s], r, sem.at[s]).start()
            return carry
        lax.fori_loop(0, MOE_TILE, body, 0, unroll=8)

    def wait(s):
        def body(r, carry):
            _row_copy(h_hbm, 0, xbuf.at[s], r, sem.at[s]).wait()
            return carry
        lax.fori_loop(0, MOE_TILE, body, 0, unroll=8)

    @pl.when((g == 0) & (n_used > 0))
    def _():
        issue(0, 0)

    @pl.when(g + 1 < n_used)
    def _():
        issue(g + 1, 1 - slot)

    @pl.when(g < n_used)
    def _():
        wait(slot)
        x = xbuf[slot].astype(BF16)
        gate = jnp.minimum(jnp.dot(x, wg_ref[0], preferred_element_type=F32) + bg_ref[0], SWIGLU_LIMIT)
        up = jnp.clip(jnp.dot(x, wu_ref[0], preferred_element_type=F32) + bu_ref[0], -SWIGLU_LIMIT, SWIGLU_LIMIT)
        act = (up + 1.0) * (gate * jax.nn.sigmoid(SWIGLU_ALPHA * gate))
        o_ref[...] = jnp.dot(act.astype(BF16), wd_ref[0], preferred_element_type=F32) + bd_ref[0]

    @pl.when(g >= n_used)
    def _():
        o_ref[...] = jnp.zeros_like(o_ref)


def _expert_ffn(h, tile_expert, n_used, row_src, wg, wu, bg, bu, wd, bd):
    d = h.shape[1]
    n_exp, _, ff = wg.shape
    n_tiles = tile_expert.shape[0]
    grid_spec = pltpu.PrefetchScalarGridSpec(
        num_scalar_prefetch=3,
        grid=(n_tiles,),
        in_specs=[pl.BlockSpec(memory_space=pl.ANY),
                  pl.BlockSpec((1, d, ff), lambda g, te, nu, rs: (te[g], 0, 0)),
                  pl.BlockSpec((1, d, ff), lambda g, te, nu, rs: (te[g], 0, 0)),
                  pl.BlockSpec((1, 1, ff), lambda g, te, nu, rs: (te[g], 0, 0)),
                  pl.BlockSpec((1, 1, ff), lambda g, te, nu, rs: (te[g], 0, 0)),
                  pl.BlockSpec((1, ff, d), lambda g, te, nu, rs: (te[g], 0, 0)),
                  pl.BlockSpec((1, 1, d), lambda g, te, nu, rs: (te[g], 0, 0))],
        out_specs=pl.BlockSpec((MOE_TILE, d), lambda g, te, nu, rs: (g, 0)),
        scratch_shapes=[pltpu.VMEM((2, MOE_TILE, d), F32), pltpu.SemaphoreType.DMA((2,))],
    )
    return pl.pallas_call(
        _expert_kernel,
        grid_spec=grid_spec,
        out_shape=jax.ShapeDtypeStruct((n_tiles * MOE_TILE, d), F32),
        compiler_params=_params(("arbitrary",)),
        name="expert_ffn",
    )(tile_expert, n_used, row_src, h, wg, wu, bg, bu, wd, bd)


def _combine_kernel(tm, final, pos_ref, x_ref, tw_ref, g_ref, fn_ref, ys_hbm, o_ref, ybuf, sem):
    i = pl.program_id(0)
    n_steps = pl.num_programs(0)
    slot = i % 2

    def issue(step, s):
        def body(r, carry):
            for k in range(TOP_K):
                _row_copy(ys_hbm, pos_ref[(step * tm + r) * TOP_K + k], ybuf.at[s, k], r, sem.at[s]).start()
            return carry
        lax.fori_loop(0, tm, body, 0, unroll=4)

    def wait(s):
        def body(r, carry):
            for k in range(TOP_K):
                _row_copy(ys_hbm, 0, ybuf.at[s, k], r, sem.at[s]).wait()
            return carry
        lax.fori_loop(0, tm, body, 0, unroll=4)

    @pl.when(i == 0)
    def _():
        issue(0, 0)

    @pl.when(i + 1 < n_steps)
    def _():
        issue(i + 1, 1 - slot)

    wait(slot)
    tw = tw_ref[...]
    acc = tw[:, 0:1] * ybuf[slot, 0]
    for k in range(1, TOP_K):
        acc = acc + tw[:, k:k + 1] * ybuf[slot, k]
    y = x_ref[...] + g_ref[0] * acc
    if final:
        y = y * lax.rsqrt(jnp.mean(y * y, axis=-1, keepdims=True) + RMS_EPS) * fn_ref[...]
    o_ref[...] = y


def _combine(x, pos, tw, ys, mod, final_gain, geo, row0, rows, final):
    n, d = x.shape
    n_ctx, seq, bsz = geo
    tm = _tile(math.gcd(n_ctx, seq), 128)
    off = row0 // tm
    gi = _mod_index(5, tm, n_ctx, seq, bsz)
    grid_spec = pltpu.PrefetchScalarGridSpec(
        num_scalar_prefetch=1,
        grid=(rows // tm,),
        in_specs=[pl.BlockSpec((tm, d), lambda i, p: (i, 0)),
                  pl.BlockSpec((tm, LANES), lambda i, p: (i, 0)),
                  pl.BlockSpec((1, 1, d), lambda i, p: (gi(i + off), 0, 0)),
                  pl.BlockSpec((1, d), lambda i, p: (0, 0)),
                  pl.BlockSpec(memory_space=pl.ANY)],
        out_specs=pl.BlockSpec((tm, d), lambda i, p: (i, 0)),
        scratch_shapes=[pltpu.VMEM((2, TOP_K, tm, d), F32), pltpu.SemaphoreType.DMA((2,))],
    )
    return pl.pallas_call(
        functools.partial(_combine_kernel, tm, final),
        grid_spec=grid_spec,
        out_shape=jax.ShapeDtypeStruct((rows, d), F32),
        compiler_params=_params(("arbitrary",)),
        name="moe_combine",
    )(pos, x, tw, mod, final_gain.reshape(1, d), ys)


def _moe_layer(x, gain, mod, router_w, router_b, wg, wu, bg, bu, wd, bd, final_gain, geo, row0, rows, final):
    n_exp = wg.shape[0]
    h, top_i, top_w = _router(x, gain, mod, router_w, router_b, geo, row0, rows)
    e_flat = top_i[:, :TOP_K].reshape(-1)
    onehot = (e_flat[:, None] == jnp.arange(n_exp, dtype=jnp.int32)[None, :]).astype(jnp.int32)
    rank = jnp.sum((jnp.cumsum(onehot, axis=0) - onehot) * onehot, axis=1)
    counts = jnp.sum(onehot, axis=0)
    padded = ((counts + MOE_TILE - 1) // MOE_TILE) * MOE_TILE
    ends = jnp.cumsum(padded)
    pos = ((ends - padded)[e_flat] + rank).astype(jnp.int32)
    n_tiles = (rows * TOP_K + n_exp * (MOE_TILE - 1)) // MOE_TILE + 1
    tok = jnp.arange(rows * TOP_K, dtype=jnp.int32) // TOP_K
    row_src = jnp.zeros((n_tiles * MOE_TILE,), jnp.int32).at[pos].set(tok, unique_indices=True)
    tile_start = jnp.arange(n_tiles, dtype=jnp.int32) * MOE_TILE
    tile_expert = jnp.minimum(jnp.searchsorted(ends, tile_start, side="right"), n_exp - 1).astype(jnp.int32)
    n_used = (ends[-1] // MOE_TILE).astype(jnp.int32).reshape(1)
    ys = _expert_ffn(h, tile_expert, n_used, row_src, wg, wu, bg, bu, wd, bd)
    return _combine(x, pos, top_w, ys, mod, final_gain, geo, row0, rows, final)


def _flip(t, rev):
    return jnp.flip(t, axis=1) if rev else t


def _linear_combine(e1, e2):
    a1, b1 = e1
    a2, b2 = e2
    return a1 * a2, a2 * b1 + b2


def _dwconv(x, w, b=None):
    k = w.shape[0]
    y = lax.conv_general_dilated(x, w[:, None, :].astype(x.dtype), (1,), [(k // 2, k - 1 - k // 2)], dimension_numbers=('NWC', 'WIO', 'NWC'), feature_group_count=x.shape[-1])
    return y if b is None else y + b.astype(x.dtype)


def _l2norm(t):
    return t * lax.rsqrt(jnp.sum(t * t, axis=-1, keepdims=True) + 1e-6)


def _rmsnorm(x, g):
    xf = x.astype(F32)
    y = xf * lax.rsqrt(jnp.mean(xf * xf, axis=-1, keepdims=True) + RMS_EPS)
    return (y * g.astype(F32)).astype(x.dtype)


def _axial_rope(n_tok, rot_dim):
    n_rows = n_tok // GRID_W
    rows = jnp.repeat(jnp.arange(n_rows), GRID_W).astype(F32)
    cols = jnp.tile(jnp.arange(GRID_W), n_rows).astype(F32)
    n_freq = rot_dim // 4
    inv = ROPE_BASE ** (-jnp.arange(n_freq, dtype=F32) / n_freq)
    ang = jnp.concatenate([rows[:, None] * inv, cols[:, None] * inv], axis=-1)
    return jnp.cos(ang), jnp.sin(ang)


def _rope(x, cos, sin):
    half = x.shape[-1] // 2
    shp = (1, cos.shape[0]) + (1,) * (x.ndim - 3) + (half,)
    c, s = cos.reshape(shp), sin.reshape(shp)
    x1, x2 = x[..., :half], x[..., half:]
    return jnp.concatenate([x1 * c - x2 * s, x1 * s + x2 * c], axis=-1).astype(x.dtype)


def _s5_scan(u, lam_bar, b_bar, c_mat, h0):
    bu = jnp.einsum('blgc,gpc->blgp', u.astype(jnp.complex64), b_bar)
    bu = bu.at[:, 0].add(lam_bar * h0)
    a = jnp.broadcast_to(lam_bar, bu.shape)
    _, h = lax.associative_scan(_linear_combine, (a, bu), axis=1)
    y = jnp.einsum('blgp,gcp->blgc', h, c_mat).real
    return y, h[:, -1]


def _s5_mixer(u_ctx, u_lat, p):
    def groups(u):
        return u.astype(F32).reshape(u.shape[0], u.shape[1], S5_GROUPS, S5_GROUP)
    uc, ul = groups(u_ctx), groups(u_lat)
    y_ctx = jnp.zeros_like(uc)
    y_lat = jnp.zeros_like(ul)
    for d, rev in enumerate((False, True)):
        lam = lax.complex(p['s5_lam_re'][d], p['s5_lam_im'][d])
        dt = jnp.exp(p['s5_log_dt'][d])[:, None]
        lam_bar = jnp.exp(lam * dt)
        b_mat = lax.complex(p['s5_b_re'][d], p['s5_b_im'][d])
        b_bar = ((lam_bar - 1.0) / lam)[..., None] * b_mat
        c_mat = lax.complex(p['s5_c_re'][d], p['s5_c_im'][d])
        h0 = jnp.zeros((uc.shape[0], S5_GROUPS, S5_STATE), jnp.complex64)
        yc, h_ctx = _s5_scan(_flip(uc, rev), lam_bar, b_bar, c_mat, h0)
        yl, _ = _s5_scan(_flip(ul, rev), lam_bar, b_bar, c_mat, h_ctx)
        y_ctx = y_ctx + _flip(yc, rev)
        y_lat = y_lat + _flip(yl, rev)
    d_skip = p['s5_d'].reshape(S5_GROUPS, S5_GROUP)
    w_glu = p['s5_w_glu']

    def readout(y, u):
        z = jax.nn.gelu((y + d_skip * u).reshape(u.shape[0], u.shape[1], BRANCH_W))
        return z * jax.nn.sigmoid(z @ w_glu)
    return readout(y_ctx, uc), readout(y_lat, ul)


def _gated_delta_rule(q, k, v, g, beta, s0):
    bsz, n_tok, n_h, _ = q.shape
    dv = v.shape[-1]
    n_chunk = n_tok // DN_CHUNK

    def chunks(t):
        return jnp.moveaxis(t.reshape(bsz, n_chunk, DN_CHUNK, n_h, -1), (1, 3), (0, 2))
    qc, kc, vc = chunks(q), chunks(k), chunks(v)
    gcum = jnp.cumsum(chunks(g[..., None])[..., 0], axis=-1)
    bc = chunks(beta[..., None])[..., 0]
    causal = jnp.tril(jnp.ones((DN_CHUNK, DN_CHUNK), bool))
    strict = jnp.tril(jnp.ones((DN_CHUNK, DN_CHUNK), bool), -1)
    decay = jnp.exp(jnp.where(causal, gcum[..., :, None] - gcum[..., None, :], -jnp.inf))
    kb = kc * bc[..., None]
    a_low = jnp.where(strict, jnp.einsum('nbhid,nbhjd->nbhij', kb, kc) * decay, 0.0)
    eye = jnp.eye(DN_CHUNK, dtype=F32)
    t_inv = lax.linalg.triangular_solve(a_low + eye, jnp.broadcast_to(eye, a_low.shape), left_side=True, lower=True, unit_diagonal=True)
    u = t_inv @ (vc * bc[..., None])
    w = t_inv @ (kb * jnp.exp(gcum)[..., None])
    qk = jnp.einsum('nbhid,nbhjd->nbhij', qc, kc) * decay

    def step(s, inp):
        q_i, k_i, u_i, w_i, g_i, qk_i = inp
        v_new = u_i - w_i @ s
        o = (q_i * jnp.exp(g_i)[..., None]) @ s + qk_i @ v_new
        g_last = g_i[..., -1:]
        k_dec = k_i * jnp.exp(g_last - g_i)[..., None]
        s = s * jnp.exp(g_last)[..., None] + jnp.einsum('bhcd,bhce->bhde', k_dec, v_new)
        return s, o
    s_fin, o = lax.scan(step, s0, (qc, kc, u, w, gcum, qk))
    o = jnp.moveaxis(o, (0, 2), (1, 3)).reshape(bsz, n_tok, n_h, dv)
    return o, s_fin


def _dn_prep(qkv, a, b, p):
    bsz, n = qkv.shape[:2]
    qkv = jax.nn.silu(_dwconv(qkv, p['dn_conv_w'])).reshape(bsz, n, 3, DN_HEADS, DN_HEAD_DIM)
    q = _l2norm(qkv[:, :, 0]) * DN_HEAD_DIM ** -0.5
    k = _l2norm(qkv[:, :, 1])
    v = qkv[:, :, 2]
    a = a.reshape(bsz, n, 2, DN_HEADS)
    g = -jnp.exp(p['dn_a_log']) * jax.nn.softplus(a + p['dn_dt_bias'])
    beta = jax.nn.sigmoid(b.reshape(bsz, n, 2, DN_HEADS))
    return q, k, v, g, beta


def _deltanet_mixer(qkv_c, a_c, b_c, z_c, qkv_l, a_l, b_l, z_l, p):
    ctx_in = _dn_prep(qkv_c, a_c, b_c, p)
    lat_in = _dn_prep(qkv_l, a_l, b_l, p)
    bsz = qkv_l.shape[0]
    o_ctx = jnp.zeros_like(ctx_in[2])
    o_lat = jnp.zeros_like(lat_in[2])
    for d, rev in enumerate((False, True)):
        def dir_inputs(t):
            q, k, v, g, beta = t
            return [_flip(z, rev) for z in (q, k, v, g[:, :, d], beta[:, :, d])]
        s0 = jnp.zeros((bsz, DN_HEADS, DN_HEAD_DIM, DN_HEAD_DIM), F32)
        oc, s_ctx = _gated_delta_rule(*dir_inputs(ctx_in), s0)
        ol, _ = _gated_delta_rule(*dir_inputs(lat_in), s_ctx)
        o_ctx = o_ctx + _flip(oc, rev)
        o_lat = o_lat + _flip(ol, rev)

    def readout(o, z):
        y = _rmsnorm(o, p['dn_norm']) * jax.nn.silu(z.reshape(o.shape))
        return y.reshape(o.shape[0], o.shape[1], BRANCH_W)
    return readout(o_ctx, z_c), readout(o_lat, z_l)


def _diff_attn_mixer(q_c, k_c, v_c, q_l, k_l, v_l, p, lam_init):
    bsz, n_lat = q_l.shape[:2]

    def qk_heads(t):
        return t.reshape(t.shape[0], t.shape[1], DA_HEADS, 2, DA_HEAD_DIM)

    def v_heads(t):
        return t.reshape(t.shape[0], t.shape[1], DA_HEADS, DA_V_DIM)
    cos, sin = _axial_rope(n_lat, DA_HEAD_DIM)
    qc, kc, vc = qk_heads(q_c), qk_heads(k_c), v_heads(v_c)
    ql, kl, vl = _rope(qk_heads(q_l), cos, sin), _rope(qk_heads(k_l), cos, sin), v_heads(v_l)
    lam = (jnp.exp(jnp.sum(p['da_lam_q1'] * p['da_lam_k1'])) - jnp.exp(jnp.sum(p['da_lam_q2'] * p['da_lam_k2'])) + lam_init)
    scale = DA_HEAD_DIM ** -0.5

    def attend(q, k, v):
        s = jnp.einsum('bqhmd,bkhmd->bhmqk', q, k) * scale
        pr = jax.nn.softmax(s, axis=-1)
        w = pr[:, :, 0] - lam * pr[:, :, 1]
        return jnp.einsum('bhqk,bkhe->bqhe', w, v)
    o_ctx = attend(qc, kc, vc)
    k_all = jnp.concatenate([kl, kc], axis=1)
    v_all = jnp.concatenate([vl, vc], axis=1)
    n_blk = n_lat // Q_BLOCK
    q_blocks = jnp.moveaxis(ql.reshape(bsz, n_blk, Q_BLOCK, DA_HEADS, 2, DA_HEAD_DIM), 1, 0)
    o_lat = lax.map(lambda qb: attend(qb, k_all, v_all), q_blocks)
    o_lat = jnp.moveaxis(o_lat, 0, 1).reshape(bsz, n_lat, DA_HEADS, DA_V_DIM)

    def readout(o):
        y = _rmsnorm(o, p['da_subln']) * (1.0 - lam_init)
        return y.reshape(o.shape[0], o.shape[1], BRANCH_W)
    return readout(o_ctx), readout(o_lat)


def _rglru_scan(x, w_r, b_r, w_i, b_i, lam, h0):
    bsz, n, _ = x.shape
    xb = x.reshape(bsz, n, LRU_BLOCKS, LRU_BLOCK_W)
    r = jax.nn.sigmoid(jnp.einsum('blhi,hij->blhj', xb, w_r).reshape(bsz, n, BRANCH_W) + b_r)
    i = jax.nn.sigmoid(jnp.einsum('blhi,hij->blhj', xb, w_i).reshape(bsz, n, BRANCH_W) + b_i)
    log_a = -LRU_C * r * jax.nn.softplus(-lam)
    a = jnp.exp(log_a)
    b = jnp.sqrt(-jnp.expm1(2.0 * log_a)) * (i * x)
    b = b.at[:, 0].add(a[:, 0] * h0)
    _, h = lax.associative_scan(_linear_combine, (a, b), axis=1)
    return h, h[:, -1]


def _rglru_mixer(x_c, g_c, x_l, g_l, p):
    xc = _dwconv(x_c, p['lru_conv_w'], p['lru_conv_b'])
    xl = _dwconv(x_l, p['lru_conv_w'], p['lru_conv_b'])
    h_ctx = jnp.zeros_like(xc)
    h_lat = jnp.zeros_like(xl)
    for d, rev in enumerate((False, True)):
        prm = tuple(p[name][d] for name in ('lru_w_rg', 'lru_b_rg', 'lru_w_ig', 'lru_b_ig', 'lru_lam'))
        h0 = jnp.zeros((xc.shape[0], BRANCH_W), F32)
        hc, h_fin = _rglru_scan(_flip(xc, rev), *prm, h0)
        hl, _ = _rglru_scan(_flip(xl, rev), *prm, h_fin)
        h_ctx = h_ctx + _flip(hc, rev)
        h_lat = h_lat + _flip(hl, rev)
    return h_ctx * jax.nn.gelu(g_c), h_lat * jax.nn.gelu(g_l)


def _mixers(proj, p, lam_init, bsz, n_ctx_tok, seq):
    w = BRANCH_W
    n_ctx = bsz * n_ctx_tok

    def cols(lo, hi):
        return proj[:n_ctx, lo:hi].reshape(bsz, n_ctx_tok, hi - lo), proj[n_ctx:, lo:hi].reshape(bsz, seq, hi - lo)
    u_c, u_l = cols(0, w)
    qkv_c, qkv_l = cols(w, 4 * w)
    z_c, z_l = cols(4 * w, 5 * w)
    q_c, q_l = cols(5 * w, 6 * w)
    k_c, k_l = cols(6 * w, 7 * w)
    v_c, v_l = cols(7 * w, 8 * w)
    x_c, x_l = cols(8 * w, 9 * w)
    g_c, g_l = cols(9 * w, 10 * w)
    a_c, a_l = cols(10 * w, 10 * w + 2 * DN_HEADS)
    b_c, b_l = cols(10 * w + 2 * DN_HEADS, 10 * w + 4 * DN_HEADS)
    outs = (_s5_mixer(u_c, u_l, p),
            _deltanet_mixer(qkv_c, a_c, b_c, z_c, qkv_l, a_l, b_l, z_l, p),
            _diff_attn_mixer(q_c, k_c, v_c, q_l, k_l, v_l, p, lam_init),
            _rglru_mixer(x_c, g_c, x_l, g_l, p))
    return [jnp.concatenate([oc.reshape(n_ctx, w), ol.reshape(bsz * seq, w)], axis=0).astype(BF16) for oc, ol in outs]


def kernel(x, c, ctx, c_ctx, ada_w, ada_b, mix_norm, ffn_norm, w_in, s5_lam_re, s5_lam_im, s5_log_dt, s5_b_re, s5_b_im, s5_c_re, s5_c_im, s5_d, s5_w_glu, dn_conv_w, dn_a_log, dn_dt_bias, dn_norm, da_lam_q1, da_lam_k1, da_lam_q2, da_lam_k2, da_subln, lru_conv_w, lru_conv_b, lru_w_rg, lru_b_rg, lru_w_ig, lru_b_ig, lru_lam, w_branch, w_out, router_w, router_b, moe_w_gu, moe_b_gu, moe_w_down, moe_b_down, final_norm):
    bsz, seq, d = x.shape
    n_ctx_tok = ctx.shape[1]
    depth = ada_w.shape[0]
    n_ctx = bsz * n_ctx_tok
    n_lat = bsz * seq
    n_tok = n_ctx + n_lat
    geo = (n_ctx, seq, bsz)
    w = BRANCH_W
    mix_cols = 10 * w + 4 * DN_HEADS
    ab_lo = 5 * w

    xs = jnp.concatenate([ctx.reshape(n_ctx, d), x.reshape(n_lat, d)], axis=0)
    mods = _ada_table(c, c_ctx, ada_w, ada_b)

    for l in range(depth):
        with_ctx = l < depth - 1
        lam_init = 0.8 - 0.6 * math.exp(-0.3 * l)
        row0, rows = (0, n_tok) if with_ctx else (n_ctx, n_lat)
        mod = mods[l]
        p = {'s5_lam_re': s5_lam_re[l], 's5_lam_im': s5_lam_im[l], 's5_log_dt': s5_log_dt[l], 's5_b_re': s5_b_re[l], 's5_b_im': s5_b_im[l], 's5_c_re': s5_c_re[l], 's5_c_im': s5_c_im[l], 's5_d': s5_d[l], 's5_w_glu': s5_w_glu[l], 'dn_conv_w': dn_conv_w[l], 'dn_a_log': dn_a_log[l], 'dn_dt_bias': dn_dt_bias[l], 'dn_norm': dn_norm[l], 'da_lam_q1': da_lam_q1[l], 'da_lam_k1': da_lam_k1[l], 'da_lam_q2': da_lam_q2[l], 'da_lam_k2': da_lam_k2[l], 'da_subln': da_subln[l], 'lru_conv_w': lru_conv_w[l], 'lru_conv_b': lru_conv_b[l], 'lru_w_rg': lru_w_rg[l], 'lru_b_rg': lru_b_rg[l], 'lru_w_ig': lru_w_ig[l], 'lru_b_ig': lru_b_ig[l], 'lru_lam': lru_lam[l]}
        wl = w_in[l]
        w_mix = jnp.concatenate([wl[:, :ab_lo], wl[:, ab_lo + 4 * DN_HEADS:mix_cols], wl[:, ab_lo:ab_lo + 4 * DN_HEADS],
                                 jnp.zeros((d, AB_COLS - 4 * DN_HEADS), F32)], axis=1).astype(BF16)
        w_gate = wl[:, mix_cols:].astype(BF16)
        wg = moe_w_gu[l][:, :, 0::2].astype(BF16)
        wu = moe_w_gu[l][:, :, 1::2].astype(BF16)
        bg = moe_b_gu[l][:, None, 0::2]
        bu = moe_b_gu[l][:, None, 1::2]
        wd = moe_w_down[l].astype(BF16)
        bd = moe_b_down[l][:, None, :]

        h = _norm_mod(xs, mix_norm[l], mod, geo)
        proj = _matmul(h, w_mix, F32)
        branches = _mixers(proj, p, lam_init, bsz, n_ctx_tok, seq)
        merged = _merge(h, branches, w_gate, w_branch[l].astype(BF16), row0, rows)
        xs = _outproj_residual(merged, w_out[l].astype(BF16), xs, mod, geo, row0, rows)
        final = l == depth - 1
        xs = _moe_layer(xs, ffn_norm[l], mod, router_w[l], router_b[l], wg, wu, bg, bu, wd, bd, final_norm, geo, row0, rows, final)
    return xs.reshape(bsz, seq, d)
```

```python
import functools
import math

import jax
import jax.numpy as jnp
import numpy as np
from jax import lax
from jax.experimental import pallas as pl
from jax.experimental.pallas import tpu as pltpu

F32 = jnp.float32
BF16 = jnp.bfloat16

GRID_W = 64
N_BRANCH = 4
BRANCH_W = 1024
RMS_EPS = 1e-6
N_MOD = 6
ROPE_BASE = 10000.0
S5_GROUP = 16
S5_GROUPS = BRANCH_W // S5_GROUP
S5_STATE = 64
DN_HEADS = 8
DN_HEAD_DIM = BRANCH_W // DN_HEADS
DN_CHUNK = 64
DA_HEADS = 8
DA_HEAD_DIM = BRANCH_W // (2 * DA_HEADS)
DA_V_DIM = 2 * DA_HEAD_DIM
Q_BLOCK = 128
LRU_BLOCKS = 8
LRU_BLOCK_W = BRANCH_W // LRU_BLOCKS
LRU_C = 8.0
TOP_K = 4
SWIGLU_LIMIT = 7.0
SWIGLU_ALPHA = 1.702

MOD_ROWS = 8
LANES = 128
MOE_TILE = 256
AB_COLS = 128
VMEM_LIMIT = 56 << 20


def _params(sem, vmem=VMEM_LIMIT):
    return pltpu.CompilerParams(dimension_semantics=sem, vmem_limit_bytes=vmem)


def _tile(n, pref):
    t = min(n, pref)
    while n % t:
        t //= 2
    return t


def _ada_kernel(c_ref, w_ref, b_ref, o_ref):
    @pl.when(pl.program_id(2) == 0)
    def _():
        o_ref[0] = jnp.broadcast_to(b_ref[0], o_ref.shape[1:])
    c = c_ref[...]
    s = (c * jax.nn.sigmoid(c)).astype(BF16)
    o_ref[0] += jnp.dot(s, w_ref[0].astype(BF16), preferred_element_type=F32)


def _ada_table(c, c_ctx, ada_w, ada_b):
    depth, d, nm = ada_w.shape
    bsz = c.shape[0]
    cs = jnp.zeros((MOD_ROWS, d), F32).at[:bsz].set(c).at[bsz].set(c_ctx)
    tk, tn = _tile(d, 1024), _tile(nm, 2048)
    mod = pl.pallas_call(
        _ada_kernel,
        grid=(depth, nm // tn, d // tk),
        in_specs=[pl.BlockSpec((MOD_ROWS, tk), lambda l, n, k: (0, k)),
                  pl.BlockSpec((1, tk, tn), lambda l, n, k: (l, k, n)),
                  pl.BlockSpec((1, 1, tn), lambda l, n, k: (l, 0, n))],
        out_specs=pl.BlockSpec((1, MOD_ROWS, tn), lambda l, n, k: (l, 0, n)),
        out_shape=jax.ShapeDtypeStruct((depth, MOD_ROWS, nm), F32),
        compiler_params=_params(("parallel", "parallel", "arbitrary")),
        name="ada_table",
    )(cs, ada_w, ada_b.reshape(depth, 1, nm))
    return jnp.moveaxis(mod.reshape(depth, MOD_ROWS, N_MOD, d), 1, 2).reshape(depth, N_MOD * MOD_ROWS, 1, d)


def _mod_index(which, tm, geo):
    n_lat, seq, bsz, _ = geo

    def idx(i):
        r = i * tm
        return which * MOD_ROWS + jnp.where(r < n_lat, r // seq, bsz)
    return idx


def _row_tile(geo, pref):
    return _tile(math.gcd(geo[1], geo[3]), pref)


def _norm_mod_kernel(x_ref, g_ref, sh_ref, sc_ref, o_ref):
    x = x_ref[...]
    y = x * lax.rsqrt(jnp.mean(x * x, axis=-1, keepdims=True) + RMS_EPS) * g_ref[...]
    o_ref[...] = (y * (1.0 + sc_ref[0]) + sh_ref[0]).astype(o_ref.dtype)


def _norm_mod(x, gain, mod, geo):
    n, d = x.shape
    tm = _row_tile(geo,256)
    sh, sc = _mod_index(0, tm, geo), _mod_index(1, tm, geo)
    return pl.pallas_call(
        _norm_mod_kernel,
        grid=(n // tm,),
        in_specs=[pl.BlockSpec((tm, d), lambda i: (i, 0)),
                  pl.BlockSpec((1, d), lambda i: (0, 0)),
                  pl.BlockSpec((1, 1, d), lambda i: (sh(i), 0, 0)),
                  pl.BlockSpec((1, 1, d), lambda i: (sc(i), 0, 0))],
        out_specs=pl.BlockSpec((tm, d), lambda i: (i, 0)),
        out_shape=jax.ShapeDtypeStruct((n, d), BF16),
        compiler_params=_params(("parallel",)),
        name="norm_mod",
    )(x, gain.reshape(1, d), mod, mod)


def _mm_kernel(a_ref, b_ref, o_ref):
    o_ref[...] = jnp.dot(a_ref[...], b_ref[...], preferred_element_type=F32).astype(o_ref.dtype)


def _matmul(a, b, out_dtype, tm_pref=512, tn_pref=1152):
    m, k = a.shape
    n = b.shape[1]
    tm = _tile(m, tm_pref)
    tn = tn_pref if n % tn_pref == 0 else _tile(n, 1024)
    return pl.pallas_call(
        _mm_kernel,
        grid=(m // tm, n // tn),
        in_specs=[pl.BlockSpec((tm, k), lambda i, j: (i, 0)),
                  pl.BlockSpec((k, tn), lambda i, j: (0, j))],
        out_specs=pl.BlockSpec((tm, tn), lambda i, j: (i, j)),
        out_shape=jax.ShapeDtypeStruct((m, n), out_dtype),
        compiler_params=_params(("parallel", "parallel")),
        name="in_proj",
    )(a, b)


def _merge_kernel(h_ref, b0, b1, b2, b3, g0, g1, g2, g3, wb_ref, o_ref):
    h = h_ref[...]
    acc = None
    for k, (br, wg) in enumerate(((b0, g0), (b1, g1), (b2, g2), (b3, g3))):
        gate = jax.nn.sigmoid(jnp.dot(h, wg[...], preferred_element_type=F32))
        lifted = jnp.dot(br[...], wb_ref[k], preferred_element_type=F32)
        acc = gate * lifted if acc is None else acc + gate * lifted
    o_ref[...] = acc.astype(o_ref.dtype)


def _merge(h, branches, w_gate, w_branch, rows):
    d = h.shape[1]
    w = branches[0].shape[1]
    tm, tn = _tile(rows, 512), _tile(d, 256)
    nt = d // tn
    gate_specs = [pl.BlockSpec((d, tn), functools.partial(lambda i, j, k: (0, k * nt + j), k=k)) for k in range(N_BRANCH)]
    return pl.pallas_call(
        _merge_kernel,
        grid=(rows // tm, nt),
        in_specs=[pl.BlockSpec((tm, d), lambda i, j: (i,0))]
        + [pl.BlockSpec((tm, w), lambda i, j: (i,0))] * N_BRANCH
        + gate_specs
        + [pl.BlockSpec((N_BRANCH, w, tn), lambda i, j: (0, 0, j))],
        out_specs=pl.BlockSpec((tm, tn), lambda i, j: (i, j)),
        out_shape=jax.ShapeDtypeStruct((rows, d), BF16),
        compiler_params=_params(("parallel", "parallel")),
        name="merge",
    )(h, *branches, w_gate, w_gate, w_gate, w_gate, w_branch)


def _outproj_kernel(m_ref, w_ref, x_ref, g_ref, o_ref):
    o_ref[...] = x_ref[...] + g_ref[0] * jnp.dot(m_ref[...], w_ref[...], preferred_element_type=F32)


def _outproj_residual(merged, w_out, x, mod, geo, rows):
    n, d = x.shape
    tm, tn = _row_tile(geo,512), _tile(d, 1024)
    gi = _mod_index(2, tm, geo)
    return pl.pallas_call(
        _outproj_kernel,
        grid=(rows // tm, d // tn),
        in_specs=[pl.BlockSpec((tm, d), lambda i, j: (i, 0)),
                  pl.BlockSpec((d, tn), lambda i, j: (0, j)),
                  pl.BlockSpec((tm, tn), lambda i, j: (i,j)),
                  pl.BlockSpec((1, 1, tn), lambda i, j: (gi(i), 0, j))],
        out_specs=pl.BlockSpec((tm, tn), lambda i, j: (i, j)),
        out_shape=jax.ShapeDtypeStruct((rows, d), F32),
        compiler_params=_params(("parallel", "parallel")),
        name="out_proj",
    )(merged, w_out, x, mod)


def _router_kernel(n_experts, x_ref, g_ref, sh_ref, sc_ref, rw_ref, rb_ref, h_ref, ti_ref, tw_ref):
    x = x_ref[...]
    y = x * lax.rsqrt(jnp.mean(x * x, axis=-1, keepdims=True) + RMS_EPS) * g_ref[...]
    h = y * (1.0 + sc_ref[0]) + sh_ref[0]
    h_ref[...] = h
    logits = jnp.dot(h.astype(BF16), rw_ref[...], preferred_element_type=F32) + rb_ref[...]
    lane = lax.broadcasted_iota(jnp.int32, logits.shape, 1)
    logits = jnp.where(lane < n_experts, logits, -jnp.inf)
    idx_out = jnp.zeros(logits.shape, jnp.int32)
    val_out = jnp.zeros(logits.shape, F32)
    top = None
    denom = None
    for k in range(TOP_K):
        m = jnp.max(logits, axis=-1, keepdims=True)
        idx = jnp.min(jnp.where(logits == m, lane, LANES), axis=-1, keepdims=True)
        top = m if top is None else top
        e = jnp.exp(m - top)
        denom = e if denom is None else denom + e
        idx_out = jnp.where(lane == k, idx, idx_out)
        val_out = jnp.where(lane == k, e, val_out)
        logits = jnp.where(lane == idx, -jnp.inf, logits)
    ti_ref[...] = idx_out
    tw_ref[...] = val_out / denom


def _router(x, gain, mod, router_w, router_b, geo, rows):
    n, d = x.shape
    n_experts = router_w.shape[1]
    tm = _row_tile(geo,256)
    sh, sc = _mod_index(3, tm, geo), _mod_index(4, tm, geo)
    rw = jnp.zeros((d, LANES), BF16).at[:, :n_experts].set(router_w.astype(BF16))
    rb = jnp.zeros((1, LANES), F32).at[0, :n_experts].set(router_b)
    return pl.pallas_call(
        functools.partial(_router_kernel, n_experts),
        grid=(rows // tm,),
        in_specs=[pl.BlockSpec((tm, d), lambda i: (i, 0)),
                  pl.BlockSpec((1, d), lambda i: (0, 0)),
                  pl.BlockSpec((1, 1, d), lambda i: (sh(i), 0, 0)),
                  pl.BlockSpec((1, 1, d), lambda i: (sc(i), 0, 0)),
                  pl.BlockSpec((d, LANES), lambda i: (0, 0)),
                  pl.BlockSpec((1, LANES), lambda i: (0, 0))],
        out_specs=[pl.BlockSpec((tm, d), lambda i: (i, 0)),
                   pl.BlockSpec((tm, LANES), lambda i: (i, 0)),
                   pl.BlockSpec((tm, LANES), lambda i: (i, 0))],
        out_shape=[jax.ShapeDtypeStruct((rows, d), F32),
                   jax.ShapeDtypeStruct((rows, LANES), jnp.int32),
                   jax.ShapeDtypeStruct((rows, LANES), F32)],
        compiler_params=_params(("parallel",)),
        name="router",
    )(x, gain.reshape(1, d), mod, mod, rw, rb)


def _row_copy(src_hbm, row, dst, r, sem):
    return pltpu.make_async_copy(src_hbm.at[pl.ds(row, 1)], dst.at[pl.ds(r, 1)], sem)


def _expert_kernel(te_ref, used_ref, src_ref, h_hbm, wg_ref, wu_ref, bg_ref, bu_ref, wd_ref, bd_ref, o_ref, xbuf, sem):
    g = pl.program_id(0)
    n_used = used_ref[0]
    slot = g % 2

    def issue(tile, s):
        def body(r, carry):
            _row_copy(h_hbm, src_ref[tile * MOE_TILE + r], xbuf.at[s], r, sem.at[s]).start()
            return carry
        lax.fori_loop(0, MOE_TILE, body, 0, unroll=8)

    def wait(s):
        def body(r, carry):
            _row_copy(h_hbm, 0, xbuf.at[s], r, sem.at[s]).wait()
            return carry
        lax.fori_loop(0, MOE_TILE, body, 0, unroll=8)

    @pl.when((g == 0) & (n_used > 0))
    def _():
        issue(0, 0)

    @pl.when(g + 1 < n_used)
    def _():
        issue(g + 1, 1 - slot)

    @pl.when(g < n_used)
    def _():
        wait(slot)
        x = xbuf[slot].astype(BF16)
        gate = jnp.minimum(jnp.dot(x, wg_ref[0], preferred_element_type=F32) + bg_ref[0], SWIGLU_LIMIT)
        up = jnp.clip(jnp.dot(x, wu_ref[0], preferred_element_type=F32) + bu_ref[0], -SWIGLU_LIMIT, SWIGLU_LIMIT)
        act = (up + 1.0) * (gate * jax.nn.sigmoid(SWIGLU_ALPHA * gate))
        o_ref[...] = jnp.dot(act.astype(BF16), wd_ref[0], preferred_element_type=F32) + bd_ref[0]

    @pl.when(g >= n_used)
    def _():
        o_ref[...] = jnp.zeros_like(o_ref)


def _expert_ffn(h, tile_expert, n_used, row_src, wg, wu, bg, bu, wd, bd):
    d = h.shape[1]
    n_exp, _, ff = wg.shape
    n_tiles = tile_expert.shape[0]
    grid_spec = pltpu.PrefetchScalarGridSpec(
        num_scalar_prefetch=3,
        grid=(n_tiles,),
        in_specs=[pl.BlockSpec(memory_space=pl.ANY),
                  pl.BlockSpec((1, d, ff), lambda g, te, nu, rs: (te[g], 0, 0)),
                  pl.BlockSpec((1, d, ff), lambda g, te, nu, rs: (te[g], 0, 0)),
                  pl.BlockSpec((1, 1, ff), lambda g, te, nu, rs: (te[g], 0, 0)),
                  pl.BlockSpec((1, 1, ff), lambda g, te, nu, rs: (te[g], 0, 0)),
                  pl.BlockSpec((1, ff, d), lambda g, te, nu, rs: (te[g], 0, 0)),
                  pl.BlockSpec((1, 1, d), lambda g, te, nu, rs: (te[g], 0, 0))],
        out_specs=pl.BlockSpec((MOE_TILE, d), lambda g, te, nu, rs: (g, 0)),
        scratch_shapes=[pltpu.VMEM((2, MOE_TILE, d), F32), pltpu.SemaphoreType.DMA((2,))],
    )
    return pl.pallas_call(
        _expert_kernel,
        grid_spec=grid_spec,
        out_shape=jax.ShapeDtypeStruct((n_tiles * MOE_TILE, d), F32),
        compiler_params=_params(("arbitrary",)),
        name="expert_ffn",
    )(tile_expert, n_used, row_src, h, wg, wu, bg, bu, wd, bd)


def _combine_kernel(tm, final, pos_ref, x_ref, tw_ref, g_ref, fn_ref, ys_hbm, o_ref, ybuf, sem):
    i = pl.program_id(0)
    n_steps = pl.num_programs(0)
    slot = i % 2

    def issue(step, s):
        def body(r, carry):
            for k in range(TOP_K):
                _row_copy(ys_hbm, pos_ref[(step * tm + r) * TOP_K + k], ybuf.at[s, k], r, sem.at[s]).start()
            return carry
        lax.fori_loop(0, tm, body, 0, unroll=4)

    def wait(s):
        def body(r, carry):
            for k in range(TOP_K):
                _row_copy(ys_hbm, 0, ybuf.at[s, k], r, sem.at[s]).wait()
            return carry
        lax.fori_loop(0, tm, body, 0, unroll=4)

    @pl.when(i == 0)
    def _():
        issue(0, 0)

    @pl.when(i + 1 < n_steps)
    def _():
        issue(i + 1, 1 - slot)

    wait(slot)
    tw = tw_ref[...]
    acc = tw[:, 0:1] * ybuf[slot, 0]
    for k in range(1, TOP_K):
        acc = acc + tw[:, k:k + 1] * ybuf[slot, k]
    y = x_ref[...] + g_ref[0] * acc
    if final:
        y = y * lax.rsqrt(jnp.mean(y * y, axis=-1, keepdims=True) + RMS_EPS) * fn_ref[...]
    o_ref[...] = y


def _combine(x, pos, tw, ys, mod, final_gain, geo, rows, final):
    n, d = x.shape
    tm = _row_tile(geo,128)
    gi = _mod_index(5, tm, geo)
    grid_spec = pltpu.PrefetchScalarGridSpec(
        num_scalar_prefetch=1,
        grid=(rows // tm,),
        in_specs=[pl.BlockSpec((tm, d), lambda i, p: (i, 0)),
                  pl.BlockSpec((tm, LANES), lambda i, p: (i, 0)),
                  pl.BlockSpec((1, 1, d), lambda i, p: (gi(i), 0, 0)),
                  pl.BlockSpec((1, d), lambda i, p: (0, 0)),
                  pl.BlockSpec(memory_space=pl.ANY)],
        out_specs=pl.BlockSpec((tm, d), lambda i, p: (i, 0)),
        scratch_shapes=[pltpu.VMEM((2, TOP_K, tm, d), F32), pltpu.SemaphoreType.DMA((2,))],
    )
    return pl.pallas_call(
        functools.partial(_combine_kernel, tm, final),
        grid_spec=grid_spec,
        out_shape=jax.ShapeDtypeStruct((rows, d), F32),
        compiler_params=_params(("arbitrary",)),
        name="moe_combine",
    )(pos, x, tw, mod, final_gain.reshape(1, d), ys)


def _moe_layer(x, gain, mod, router_w, router_b, wg, wu, bg, bu, wd, bd, final_gain, geo, rows, final):
    n_exp = wg.shape[0]
    h, top_i, top_w = _router(x, gain, mod, router_w, router_b, geo, rows)
    e_flat = top_i[:, :TOP_K].reshape(-1)
    onehot = (e_flat[:, None] == jnp.arange(n_exp, dtype=jnp.int32)[None, :]).astype(jnp.int32)
    rank = jnp.sum((jnp.cumsum(onehot, axis=0) - onehot) * onehot, axis=1)
    counts = jnp.sum(onehot, axis=0)
    padded = ((counts + MOE_TILE - 1) // MOE_TILE) * MOE_TILE
    ends = jnp.cumsum(padded)
    pos = ((ends - padded)[e_flat] + rank).astype(jnp.int32)
    n_tiles = (rows * TOP_K + n_exp * (MOE_TILE - 1)) // MOE_TILE + 1
    tok = jnp.arange(rows * TOP_K, dtype=jnp.int32) // TOP_K
    row_src = jnp.zeros((n_tiles * MOE_TILE,), jnp.int32).at[pos].set(tok, unique_indices=True)
    tile_start = jnp.arange(n_tiles, dtype=jnp.int32) * MOE_TILE
    tile_expert = jnp.minimum(jnp.searchsorted(ends, tile_start, side="right"), n_exp - 1).astype(jnp.int32)
    n_used = (ends[-1] // MOE_TILE).astype(jnp.int32).reshape(1)
    ys = _expert_ffn(h, tile_expert, n_used, row_src, wg, wu, bg, bu, wd, bd)
    return _combine(x, pos, top_w, ys, mod, final_gain, geo, rows, final)


def _flip(t, rev):
    return jnp.flip(t, axis=1) if rev else t


def _linear_combine(e1, e2):
    a1, b1 = e1
    a2, b2 = e2
    return a1 * a2, a2 * b1 + b2


def _dwconv(x, w, b=None):
    k = w.shape[0]
    y = lax.conv_general_dilated(x, w[:, None, :].astype(x.dtype), (1,), [(k // 2, k - 1 - k // 2)], dimension_numbers=('NWC', 'WIO', 'NWC'), feature_group_count=x.shape[-1])
    return y if b is None else y + b.astype(x.dtype)


def _l2norm(t):
    return t * lax.rsqrt(jnp.sum(t * t, axis=-1, keepdims=True) + 1e-6)


def _rmsnorm(x, g):
    xf = x.astype(F32)
    y = xf * lax.rsqrt(jnp.mean(xf * xf, axis=-1, keepdims=True) + RMS_EPS)
    return (y * g.astype(F32)).astype(x.dtype)


def _axial_rope(n_tok, rot_dim):
    n_rows = n_tok // GRID_W
    rows = jnp.repeat(jnp.arange(n_rows), GRID_W).astype(F32)
    cols = jnp.tile(jnp.arange(GRID_W), n_rows).astype(F32)
    n_freq = rot_dim // 4
    inv = ROPE_BASE ** (-jnp.arange(n_freq, dtype=F32) / n_freq)
    ang = jnp.concatenate([rows[:, None] * inv, cols[:, None] * inv], axis=-1)
    return jnp.cos(ang), jnp.sin(ang)


def _rope(x, cos, sin):
    half = x.shape[-1] // 2
    shp = (1, cos.shape[0]) + (1,) * (x.ndim - 3) + (half,)
    c, s = cos.reshape(shp), sin.reshape(shp)
    x1, x2 = x[..., :half], x[..., half:]
    return jnp.concatenate([x1 * c - x2 * s, x1 * s + x2 * c], axis=-1).astype(x.dtype)


def _s5_scan(u, lam_bar, b_bar, c_mat, h0):
    bu = jnp.einsum('blgc,gpc->blgp', u.astype(jnp.complex64), b_bar)
    bu = bu.at[:, 0].add(lam_bar * h0)
    a = jnp.broadcast_to(lam_bar, bu.shape)
    _, h = lax.associative_scan(_linear_combine, (a, bu), axis=1)
    y = jnp.einsum('blgp,gcp->blgc', h, c_mat).real
    return y, h[:, -1]


def _s5_mixer(u_ctx, u_lat, p):
    def groups(u):
        return u.astype(F32).reshape(u.shape[0], u.shape[1], S5_GROUPS, S5_GROUP)
    uc, ul = groups(u_ctx), groups(u_lat)
    y_ctx = jnp.zeros_like(uc)
    y_lat = jnp.zeros_like(ul)
    for d, rev in enumerate((False, True)):
        lam = lax.complex(p['s5_lam_re'][d], p['s5_lam_im'][d])
        dt = jnp.exp(p['s5_log_dt'][d])[:, None]
        lam_bar = jnp.exp(lam * dt)
        b_mat = lax.complex(p['s5_b_re'][d], p['s5_b_im'][d])
        b_bar = ((lam_bar - 1.0) / lam)[..., None] * b_mat
        c_mat = lax.complex(p['s5_c_re'][d], p['s5_c_im'][d])
        h0 = jnp.zeros((uc.shape[0], S5_GROUPS, S5_STATE), jnp.complex64)
        yc, h_ctx = _s5_scan(_flip(uc, rev), lam_bar, b_bar, c_mat, h0)
        yl, _ = _s5_scan(_flip(ul, rev), lam_bar, b_bar, c_mat, h_ctx)
        y_ctx = y_ctx + _flip(yc, rev)
        y_lat = y_lat + _flip(yl, rev)
    d_skip = p['s5_d'].reshape(S5_GROUPS, S5_GROUP)
    w_glu = p['s5_w_glu']

    def readout(y, u):
        z = jax.nn.gelu((y + d_skip * u).reshape(u.shape[0], u.shape[1], BRANCH_W))
        return z * jax.nn.sigmoid(z @ w_glu)
    return readout(y_ctx, uc), readout(y_lat, ul)


def _gated_delta_rule(q, k, v, g, beta, s0):
    bsz, n_tok, n_h, _ = q.shape
    dv = v.shape[-1]
    n_chunk = n_tok // DN_CHUNK

    def chunks(t):
        return jnp.moveaxis(t.reshape(bsz, n_chunk, DN_CHUNK, n_h, -1), (1, 3), (0, 2))
    qc, kc, vc = chunks(q), chunks(k), chunks(v)
    gcum = jnp.cumsum(chunks(g[..., None])[..., 0], axis=-1)
    bc = chunks(beta[..., None])[..., 0]
    causal = jnp.tril(jnp.ones((DN_CHUNK, DN_CHUNK), bool))
    strict = jnp.tril(jnp.ones((DN_CHUNK, DN_CHUNK), bool), -1)
    decay = jnp.exp(jnp.where(causal, gcum[..., :, None] - gcum[..., None, :], -jnp.inf))
    kb = kc * bc[..., None]
    a_low = jnp.where(strict, jnp.einsum('nbhid,nbhjd->nbhij', kb, kc) * decay, 0.0)
    eye = jnp.eye(DN_CHUNK, dtype=F32)
    t_inv = lax.linalg.triangular_solve(a_low + eye, jnp.broadcast_to(eye, a_low.shape), left_side=True, lower=True, unit_diagonal=True)
    u = t_inv @ (vc * bc[..., None])
    w = t_inv @ (kb * jnp.exp(gcum)[..., None])
    qk = jnp.einsum('nbhid,nbhjd->nbhij', qc, kc) * decay

    def step(s, inp):
        q_i, k_i, u_i, w_i, g_i, qk_i = inp
        v_new = u_i - w_i @ s
        o = (q_i * jnp.exp(g_i)[..., None]) @ s + qk_i @ v_new
        g_last = g_i[..., -1:]
        k_dec = k_i * jnp.exp(g_last - g_i)[..., None]
        s = s * jnp.exp(g_last)[..., None] + jnp.einsum('bhcd,bhce->bhde', k_dec, v_new)
        return s, o
    s_fin, o = lax.scan(step, s0, (qc, kc, u, w, gcum, qk))
    o = jnp.moveaxis(o, (0, 2), (1, 3)).reshape(bsz, n_tok, n_h, dv)
    return o, s_fin


def _dn_prep(qkv, a, b, p):
    bsz, n = qkv.shape[:2]
    qkv = jax.nn.silu(_dwconv(qkv, p['dn_conv_w'])).reshape(bsz, n, 3, DN_HEADS, DN_HEAD_DIM)
    q = _l2norm(qkv[:, :, 0]) * DN_HEAD_DIM ** -0.5
    k = _l2norm(qkv[:, :, 1])
    v = qkv[:, :, 2]
    a = a.reshape(bsz, n, 2, DN_HEADS)
    g = -jnp.exp(p['dn_a_log']) * jax.nn.softplus(a + p['dn_dt_bias'])
    beta = jax.nn.sigmoid(b.reshape(bsz, n, 2, DN_HEADS))
    return q, k, v, g, beta


def _deltanet_mixer(qkv_c, a_c, b_c, z_c, qkv_l, a_l, b_l, z_l, p):
    ctx_in = _dn_prep(qkv_c, a_c, b_c, p)
    lat_in = _dn_prep(qkv_l, a_l, b_l, p)
    bsz = qkv_l.shape[0]
    o_ctx = jnp.zeros_like(ctx_in[2])
    o_lat = jnp.zeros_like(lat_in[2])
    for d, rev in enumerate((False, True)):
        def dir_inputs(t):
            q, k, v, g, beta = t
            return [_flip(z, rev) for z in (q, k, v, g[:, :, d], beta[:, :, d])]
        s0 = jnp.zeros((bsz, DN_HEADS, DN_HEAD_DIM, DN_HEAD_DIM), F32)
        oc, s_ctx = _gated_delta_rule(*dir_inputs(ctx_in), s0)
        ol, _ = _gated_delta_rule(*dir_inputs(lat_in), s_ctx)
        o_ctx = o_ctx + _flip(oc, rev)
        o_lat = o_lat + _flip(ol, rev)

    def readout(o, z):
        y = _rmsnorm(o, p['dn_norm']) * jax.nn.silu(z.reshape(o.shape))
        return y.reshape(o.shape[0], o.shape[1], BRANCH_W)
    return readout(o_ctx, z_c), readout(o_lat, z_l)


def _diff_attn_mixer(q_c, k_c, v_c, q_l, k_l, v_l, p, lam_init):
    bsz, n_lat = q_l.shape[:2]

    def qk_heads(t):
        return t.reshape(t.shape[0], t.shape[1], DA_HEADS, 2, DA_HEAD_DIM)

    def v_heads(t):
        return t.reshape(t.shape[0], t.shape[1], DA_HEADS, DA_V_DIM)
    cos, sin = _axial_rope(n_lat, DA_HEAD_DIM)
    qc, kc, vc = qk_heads(q_c), qk_heads(k_c), v_heads(v_c)
    ql, kl, vl = _rope(qk_heads(q_l), cos, sin), _rope(qk_heads(k_l), cos, sin), v_heads(v_l)
    lam = (jnp.exp(jnp.sum(p['da_lam_q1'] * p['da_lam_k1'])) - jnp.exp(jnp.sum(p['da_lam_q2'] * p['da_lam_k2'])) + lam_init)
    scale = DA_HEAD_DIM ** -0.5

    def attend(q, k, v):
        s = jnp.einsum('bqhmd,bkhmd->bhmqk', q, k) * scale
        pr = jax.nn.softmax(s, axis=-1)
        w = pr[:, :, 0] - lam * pr[:, :, 1]
        return jnp.einsum('bhqk,bkhe->bqhe', w, v)
    o_ctx = attend(qc, kc, vc)
    k_all = jnp.concatenate([kl, kc], axis=1)
    v_all = jnp.concatenate([vl, vc], axis=1)
    n_blk = n_lat // Q_BLOCK
    q_blocks = jnp.moveaxis(ql.reshape(bsz, n_blk, Q_BLOCK, DA_HEADS, 2, DA_HEAD_DIM), 1, 0)
    o_lat = lax.map(lambda qb: attend(qb, k_all, v_all), q_blocks)
    o_lat = jnp.moveaxis(o_lat, 0, 1).reshape(bsz, n_lat, DA_HEADS, DA_V_DIM)

    def readout(o):
        y = _rmsnorm(o, p['da_subln']) * (1.0 - lam_init)
        return y.reshape(o.shape[0], o.shape[1], BRANCH_W)
    return readout(o_ctx), readout(o_lat)


def _rglru_scan(x, w_r, b_r, w_i, b_i, lam, h0):
    bsz, n, _ = x.shape
    xb = x.reshape(bsz, n, LRU_BLOCKS, LRU_BLOCK_W)
    r = jax.nn.sigmoid(jnp.einsum('blhi,hij->blhj', xb, w_r).reshape(bsz, n, BRANCH_W) + b_r)
    i = jax.nn.sigmoid(jnp.einsum('blhi,hij->blhj', xb, w_i).reshape(bsz, n, BRANCH_W) + b_i)
    log_a = -LRU_C * r * jax.nn.softplus(-lam)
    a = jnp.exp(log_a)
    b = jnp.sqrt(-jnp.expm1(2.0 * log_a)) * (i * x)
    b = b.at[:, 0].add(a[:, 0] * h0)
    _, h = lax.associative_scan(_linear_combine, (a, b), axis=1)
    return h, h[:, -1]


def _rglru_mixer(x_c, g_c, x_l, g_l, p):
    xc = _dwconv(x_c, p['lru_conv_w'], p['lru_conv_b'])
    xl = _dwconv(x_l, p['lru_conv_w'], p['lru_conv_b'])
    h_ctx = jnp.zeros_like(xc)
    h_lat = jnp.zeros_like(xl)
    for d, rev in enumerate((False, True)):
        prm = tuple(p[name][d] for name in ('lru_w_rg', 'lru_b_rg', 'lru_w_ig', 'lru_b_ig', 'lru_lam'))
        h0 = jnp.zeros((xc.shape[0], BRANCH_W), F32)
        hc, h_fin = _rglru_scan(_flip(xc, rev), *prm, h0)
        hl, _ = _rglru_scan(_flip(xl, rev), *prm, h_fin)
        h_ctx = h_ctx + _flip(hc, rev)
        h_lat = h_lat + _flip(hl, rev)
    return h_ctx * jax.nn.gelu(g_c), h_lat * jax.nn.gelu(g_l)


S5_CHUNK = 16
S5_ROW = S5_CHUNK * S5_GROUP
S5_GBLK = 8


def _s5_operators(p, n_chunks):
    hi = lax.Precision.HIGHEST
    n_steps = max(1, (n_chunks - 1).bit_length())
    ci = jnp.arange(S5_CHUNK)
    t_ops, in_ops, out_ops, scan_ops = [], [], [], []
    for d in range(2):
        lam = lax.complex(p['s5_lam_re'][d], p['s5_lam_im'][d])
        dt = jnp.exp(p['s5_log_dt'][d])[:, None]
        lam_bar = jnp.exp(lam * dt)
        b_bar = ((lam_bar - 1.0) / lam)[..., None] * lax.complex(p['s5_b_re'][d], p['s5_b_im'][d])
        c_mat = lax.complex(p['s5_c_re'][d], p['s5_c_im'][d])
        pw = [jnp.ones_like(lam_bar)]
        for _ in range(S5_CHUNK):
            pw.append(pw[-1] * lam_bar)
        pw = jnp.stack(pw)
        kern = jnp.einsum('gop,mgp,gpi->mgoi', c_mat, pw[:S5_CHUNK], b_bar, precision=hi).real
        lag = (ci[None, :] - ci[:, None]) if d == 0 else (ci[:, None] - ci[None, :])
        t_op = jnp.where((lag >= 0)[:, :, None, None, None], kern[jnp.clip(lag, 0, S5_CHUNK - 1)], 0.0)
        t_ops.append(jnp.transpose(t_op, (2, 0, 4, 1, 3)).reshape(S5_GROUPS, S5_ROW, S5_ROW))
        e_in = (S5_CHUNK - 1 - ci) if d == 0 else ci
        m_in = pw[e_in][:, :, :, None] * b_bar[None]
        m_in = jnp.transpose(m_in, (1, 0, 3, 2)).reshape(S5_GROUPS, S5_ROW, S5_STATE)
        in_ops.append(jnp.concatenate([m_in.real, m_in.imag], axis=-1))
        e_out = (ci + 1) if d == 0 else (S5_CHUNK - ci)
        m_out = c_mat[None] * pw[e_out][:, :, None, :]
        m_out = jnp.transpose(m_out, (1, 3, 0, 2)).reshape(S5_GROUPS, S5_STATE, S5_ROW)
        out_ops.append(jnp.concatenate([m_out.real, -m_out.imag], axis=1))
        a = pw[S5_CHUNK]
        mults = []
        for _ in range(n_steps):
            mults.append(a)
            a = a * a
        mults = jnp.stack(mults, axis=1)
        scan_ops.append(jnp.concatenate([jnp.concatenate([mults.real, mults.real], -1), jnp.concatenate([-mults.imag, mults.imag], -1)], axis=1))
    return (jnp.stack(t_ops).astype(BF16), jnp.stack(in_ops).astype(BF16), jnp.stack(out_ops).astype(BF16), jnp.stack(scan_ops))


def _s5_kernel(nb, n_steps, ul_ref, uc_ref, t_ref, in_ref, out_ref, sc_ref, yl_ref, yc_ref):
    n_l, n_c = ul_ref.shape[1], uc_ref.shape[1]

    def shift(x, rows, reverse):
        n = x.shape[0]
        row = lax.broadcasted_iota(jnp.int32, x.shape, 0)
        if reverse:
            return jnp.where(row < n - rows, pltpu.roll(x, n - rows, axis=0), 0.0)
        return jnp.where(row >= rows, pltpu.roll(x, rows, axis=0), 0.0)

    def group(g, carry):
        ul, uc = ul_ref[g], uc_ref[g]
        ys = []
        for d in range(2):
            rev = d == 1
            u = jnp.concatenate([ul, uc], axis=0) if rev else jnp.concatenate([uc, ul], axis=0)
            y = jnp.dot(u, t_ref[d, g], preferred_element_type=F32)
            x = jnp.dot(u, in_ref[d, g], preferred_element_type=F32)
            sc = sc_ref[d, g]
            for k in range(n_steps):
                if nb * 2 ** k >= x.shape[0]:
                    break
                xs = shift(x, nb * 2 ** k, rev)
                x = x + sc[k:k + 1] * xs + sc[n_steps + k:n_steps + k + 1] * pltpu.roll(xs, S5_STATE, axis=1)
            h_in = shift(x, nb, rev)
            ys.append(y + jnp.dot(h_in.astype(BF16), out_ref[d, g], preferred_element_type=F32))
        yl_ref[g] = ys[0][n_c:] + ys[1][:n_l]
        yc_ref[g] = ys[0][:n_c] + ys[1][n_l:]
        return carry
    lax.fori_loop(0, ul_ref.shape[0], group, 0)


def _s5_readout_kernel(y_ref, u_ref, d_ref, w_ref, o_ref):
    z = jax.nn.gelu(y_ref[...] + d_ref[...] * u_ref[...])
    o_ref[...] = (z * jax.nn.sigmoid(jnp.dot(z.astype(BF16), w_ref[...], preferred_element_type=F32))).astype(o_ref.dtype)


def _s5_mixer_pallas(proj, p, geo, rows):
    n_lat, seq, bsz, n_ctx = geo
    w = BRANCH_W
    nl, nc = seq // S5_CHUNK, (n_ctx // bsz) // S5_CHUNK
    n_steps = max(1, (nl + nc - 1).bit_length())
    t_op, in_op, out_op, scan_op = _s5_operators(p, nl + nc)

    def to_rows(u, n):
        u = u.reshape(bsz, n, S5_CHUNK, S5_GROUPS, S5_GROUP)
        return jnp.transpose(u, (3, 1, 0, 2, 4)).reshape(S5_GROUPS, n * bsz, S5_ROW)

    def from_rows(y, n):
        y = y.reshape(S5_GROUPS, n, bsz, S5_CHUNK, S5_GROUP)
        return jnp.transpose(y, (2, 1, 3, 0, 4)).reshape(bsz * n * S5_CHUNK, w)
    u = proj[:, :w]
    ul, uc = to_rows(u[:n_lat].astype(BF16), nl), to_rows(u[n_lat:].astype(BF16), nc)
    gb = S5_GBLK
    yl, yc = pl.pallas_call(
        functools.partial(_s5_kernel, bsz, n_steps),
        grid=(S5_GROUPS // gb,),
        in_specs=[pl.BlockSpec((gb, nl * bsz, S5_ROW), lambda i: (i, 0, 0)),
                  pl.BlockSpec((gb, nc * bsz, S5_ROW), lambda i: (i, 0, 0)),
                  pl.BlockSpec((2, gb, S5_ROW, S5_ROW), lambda i: (0, i, 0, 0)),
                  pl.BlockSpec((2, gb, S5_ROW, 2 * S5_STATE), lambda i: (0, i, 0, 0)),
                  pl.BlockSpec((2, gb, 2 * S5_STATE, S5_ROW), lambda i: (0, i, 0, 0)),
                  pl.BlockSpec((2, gb, 2 * n_steps, 2 * S5_STATE), lambda i: (0, i, 0, 0))],
        out_specs=[pl.BlockSpec((gb, nl * bsz, S5_ROW), lambda i: (i, 0, 0)),
                   pl.BlockSpec((gb, nc * bsz, S5_ROW), lambda i: (i, 0, 0))],
        out_shape=[jax.ShapeDtypeStruct((S5_GROUPS, nl * bsz, S5_ROW), F32),
                   jax.ShapeDtypeStruct((S5_GROUPS, nc * bsz, S5_ROW), F32)],
        compiler_params=_params(("parallel",)),
        name="s5_scan",
    )(ul, uc, t_op, in_op, out_op, scan_op)
    y = from_rows(yl, nl) if rows == n_lat else jnp.concatenate([from_rows(yl, nl), from_rows(yc, nc)], axis=0)
    tm = _row_tile(geo, 256)
    return pl.pallas_call(
        _s5_readout_kernel,
        grid=(rows // tm,),
        in_specs=[pl.BlockSpec((tm, w), lambda i: (i, 0)),
                  pl.BlockSpec((tm, w), lambda i: (i, 0)),
                  pl.BlockSpec((1, w), lambda i: (0, 0)),
                  pl.BlockSpec((w, w), lambda i: (0, 0))],
        out_specs=pl.BlockSpec((tm, w), lambda i: (i, 0)),
        out_shape=jax.ShapeDtypeStruct((rows, w), BF16),
        compiler_params=_params(("parallel",)),
        name="s5_readout",
    )(y, proj, p['s5_d'].reshape(1, w), p['s5_w_glu'].astype(BF16))


def _rope_tables(seq, blk):
    t = jnp.arange(seq)
    n_freq = DA_HEAD_DIM // 4
    inv = ROPE_BASE ** (-jnp.arange(n_freq, dtype=F32) / n_freq)
    ang = jnp.concatenate([(t // GRID_W).astype(F32)[:, None] * inv, (t % GRID_W).astype(F32)[:, None] * inv], axis=-1)
    cos = jnp.concatenate([jnp.cos(ang), jnp.cos(ang)], axis=-1)
    sin = jnp.concatenate([-jnp.sin(ang), jnp.sin(ang)], axis=-1)
    reps = BRANCH_W // DA_HEAD_DIM
    cos = jnp.concatenate([jnp.tile(cos, (1, reps)), jnp.ones((blk, BRANCH_W), F32)], axis=0)
    sin = jnp.concatenate([jnp.tile(sin, (1, reps)), jnp.zeros((blk, BRANCH_W), F32)], axis=0)
    return cos, sin


def _da_prep_kernel(q_ref, k_ref, v_ref, cos_ref, sin_ref, qo_ref, ko_ref, vo_ref):
    cos, sin = cos_ref[...], sin_ref[...]
    w = cos.shape[1]
    half = DA_HEAD_DIM // 2
    first = (lax.broadcasted_iota(jnp.int32, cos.shape, 1) % DA_HEAD_DIM) < half

    def rope(x):
        partner = jnp.where(first, pltpu.roll(x, w - half, axis=1), pltpu.roll(x, half, axis=1))
        return x * cos + partner * sin
    qo_ref[...] = (rope(q_ref[...]) * DA_HEAD_DIM ** -0.5).astype(qo_ref.dtype)
    ko_ref[...] = rope(k_ref[...]).astype(ko_ref.dtype)
    vo_ref[...] = v_ref[...].astype(vo_ref.dtype)


def _da_kernel(n_qb, q_ref, kl_ref, kc_ref, vl_ref, vc_ref, prm_ref, o_ref):
    lam = prm_ref[0:1, :]
    gain = prm_ref[1:2, :]
    nt = (((1,), (1,)), ((), ()))

    def attend(keys):
        q = q_ref[...]
        outs = []
        for m in range(2):
            lo, hi = m * DA_HEAD_DIM, (m + 1) * DA_HEAD_DIM
            s = [lax.dot_general(q[:, lo:hi], k_ref[:, lo:hi], nt, preferred_element_type=F32) for k_ref, _ in keys]
            mx = functools.reduce(jnp.maximum, [jnp.max(t, axis=-1, keepdims=True) for t in s])
            pr = [jnp.exp(t - mx) for t in s]
            den = functools.reduce(jnp.add, [jnp.sum(t, axis=-1, keepdims=True) for t in pr])
            num = functools.reduce(jnp.add, [jnp.dot(t.astype(BF16), v_ref[...], preferred_element_type=F32) for t, (_, v_ref) in zip(pr, keys)])
            outs.append(num / den)
        o = outs[0] - lam * outs[1]
        y = o * lax.rsqrt(jnp.mean(o * o, axis=-1, keepdims=True) + RMS_EPS) * gain
        o_ref[...] = y.astype(o_ref.dtype)

    @pl.when(pl.program_id(2) < n_qb)
    def _():
        attend([(kl_ref, vl_ref), (kc_ref, vc_ref)])

    @pl.when(pl.program_id(2) >= n_qb)
    def _():
        attend([(kc_ref, vc_ref)])


def _diff_attn_pallas(proj, p, lam_init, geo, rows):
    n_lat, seq, bsz, n_ctx = geo
    w = BRANCH_W
    n = proj.shape[0]
    tq = n_ctx // bsz
    n_qb = seq // tq
    cos, sin = _rope_tables(seq, tq)
    tab = lambda i: (jnp.where(i * tq < n_lat, i % n_qb, n_qb), 0)
    qr, kr, vb = pl.pallas_call(
        _da_prep_kernel,
        grid=(n // tq,),
        in_specs=[pl.BlockSpec((tq, w), lambda i: (i, 5)), pl.BlockSpec((tq, w), lambda i: (i, 6)), pl.BlockSpec((tq, w), lambda i: (i, 7)),
                  pl.BlockSpec((tq, w), tab), pl.BlockSpec((tq, w), tab)],
        out_specs=[pl.BlockSpec((tq, w), lambda i: (i, 0))] * 3,
        out_shape=[jax.ShapeDtypeStruct((n, w), BF16)] * 3,
        compiler_params=_params(("parallel",)),
        name="da_prep",
    )(proj, proj, proj, cos, sin)
    lam = jnp.exp(jnp.sum(p['da_lam_q1'] * p['da_lam_k1'])) - jnp.exp(jnp.sum(p['da_lam_q2'] * p['da_lam_k2'])) + lam_init
    prm = jnp.zeros((8, DA_V_DIM), F32).at[0].set(lam).at[1].set(p['da_subln'] * (1.0 - lam_init))
    with_ctx = rows > n_lat
    ctx_blk = n_lat // tq
    qrow = lambda b, h, i: (jnp.where(i < n_qb, b * n_qb + i, ctx_blk + b), h)
    return pl.pallas_call(
        functools.partial(_da_kernel, n_qb),
        grid=(bsz, DA_HEADS, n_qb + (1 if with_ctx else 0)),
        in_specs=[pl.BlockSpec((tq, DA_V_DIM), qrow),
                  pl.BlockSpec((seq, DA_V_DIM), lambda b, h, i: (b, h)),
                  pl.BlockSpec((tq, DA_V_DIM), lambda b, h, i: (ctx_blk + b, h)),
                  pl.BlockSpec((seq, DA_V_DIM), lambda b, h, i: (b, h)),
                  pl.BlockSpec((tq, DA_V_DIM), lambda b, h, i: (ctx_blk + b, h)),
                  pl.BlockSpec((8, DA_V_DIM), lambda b, h, i: (0, 0))],
        out_specs=pl.BlockSpec((tq, DA_V_DIM), qrow),
        out_shape=jax.ShapeDtypeStruct((rows, w), BF16),
        compiler_params=_params(("parallel", "parallel", "arbitrary")),
        name="diff_attn",
    )(qr, kr, kr, vb, vb, prm)


LRU_CBLK = 256
SUBLANES = 8


def _lru_kernel(xl_ref, xc_ref, gl_ref, gc_ref, cw_ref, cb_ref, wr_ref, br_ref, wi_ref, bi_ref, lam_ref, ol_ref, oc_ref, a_sc, b_sc, h_sc):
    n_l, n_c = xl_ref.shape[0], xc_ref.shape[0]
    n_all = n_l + n_c
    cw, cb = cw_ref[...], cb_ref[...]

    def conv(x):
        n = x.shape[0]
        row = lax.broadcasted_iota(jnp.int32, x.shape, 0)
        xm2 = jnp.where(row >= 2, pltpu.roll(x, 2, axis=0), 0.0)
        xm1 = jnp.where(row >= 1, pltpu.roll(x, 1, axis=0), 0.0)
        xp1 = jnp.where(row < n - 1, pltpu.roll(x, n - 1, axis=0), 0.0)
        return cw[0:1] * xm2 + cw[1:2] * xm1 + cw[2:3] * x + cw[3:4] * xp1 + cb
    xc = jnp.concatenate([conv(xc_ref[...]), conv(xl_ref[...])], axis=0)
    xb = xc.astype(BF16)
    sub = lax.broadcasted_iota(jnp.int32, xc.shape, 0) % SUBLANES
    n_blk = xc.shape[1] // LRU_BLOCK_W

    def gate(w_ref, b_ref, d):
        cols = [jnp.dot(xb[:, k * LRU_BLOCK_W:(k + 1) * LRU_BLOCK_W], w_ref[d, k], preferred_element_type=F32) for k in range(n_blk)]
        return jax.nn.sigmoid(jnp.concatenate(cols, axis=1) + b_ref[d])

    for d in range(2):
        rev = d == 1
        r, i = gate(wr_ref, br_ref, d), gate(wi_ref, bi_ref, d)
        log_a = -LRU_C * r * jax.nn.softplus(-lam_ref[d])
        a = jnp.exp(log_a)
        b = jnp.sqrt(1.0 - a * a) * (i * xc)
        for s in (1, 2, 4):
            if rev:
                keep = sub < SUBLANES - s
                a_n = jnp.where(keep, pltpu.roll(a, n_all - s, axis=0), 1.0)
                b_n = jnp.where(keep, pltpu.roll(b, n_all - s, axis=0), 0.0)
            else:
                keep = sub >= s
                a_n = jnp.where(keep, pltpu.roll(a, s, axis=0), 1.0)
                b_n = jnp.where(keep, pltpu.roll(b, s, axis=0), 0.0)
            b = b + a * b_n
            a = a * a_n
        a_sc[...] = a
        b_sc[...] = b
        last = 0 if rev else SUBLANES - 1

        def tile_step(t, carry):
            rows = pl.ds(pl.multiple_of(t * SUBLANES, SUBLANES), SUBLANES)
            h = b_sc[rows, :] + a_sc[rows, :] * carry
            if rev:
                h_sc[rows, :] += h
            else:
                h_sc[rows, :] = h
            return jnp.broadcast_to(h[last:last + 1], h.shape)
        carry = jnp.zeros((SUBLANES, xc.shape[1]), F32)
        tc, ta = n_c // SUBLANES, n_all // SUBLANES
        if rev:
            carry = lax.fori_loop(0, tc, lambda t, c: tile_step(tc - 1 - t, c), carry)
            lax.fori_loop(0, ta - tc, lambda t, c: tile_step(ta - 1 - t, c), carry)
        else:
            lax.fori_loop(0, ta, tile_step, carry)
    oc_ref[...] = (h_sc[0:n_c, :] * jax.nn.gelu(gc_ref[...])).astype(oc_ref.dtype)
    ol_ref[...] = (h_sc[n_c:n_all, :] * jax.nn.gelu(gl_ref[...])).astype(ol_ref.dtype)


def _rglru_pallas(proj, p, geo):
    n_lat, seq, bsz, n_ctx = geo
    w = BRANCH_W
    ctx_len = n_ctx // bsz
    cblk = LRU_CBLK
    ncb = w // cblk
    gpb = cblk // LRU_BLOCK_W
    lat_rows, ctx_rows = seq // ctx_len, n_lat // ctx_len
    xcol, gcol = 8 * ncb, 9 * ncb
    vec = lambda t: t.reshape(2, 1, w)
    return pl.pallas_call(
        _lru_kernel,
        grid=(bsz, ncb),
        in_specs=[pl.BlockSpec((seq, cblk), lambda b, c: (b, xcol + c)),
                  pl.BlockSpec((ctx_len, cblk), lambda b, c: (ctx_rows + b, xcol + c)),
                  pl.BlockSpec((seq, cblk), lambda b, c: (b, gcol + c)),
                  pl.BlockSpec((ctx_len, cblk), lambda b, c: (ctx_rows + b, gcol + c)),
                  pl.BlockSpec((4, cblk), lambda b, c: (0, c)),
                  pl.BlockSpec((1, cblk), lambda b, c: (0, c)),
                  pl.BlockSpec((2, gpb, LRU_BLOCK_W, LRU_BLOCK_W), lambda b, c: (0, c, 0, 0)),
                  pl.BlockSpec((2, 1, cblk), lambda b, c: (0, 0, c)),
                  pl.BlockSpec((2, gpb, LRU_BLOCK_W, LRU_BLOCK_W), lambda b, c: (0, c, 0, 0)),
                  pl.BlockSpec((2, 1, cblk), lambda b, c: (0, 0, c)),
                  pl.BlockSpec((2, 1, cblk), lambda b, c: (0, 0, c))],
        out_specs=[pl.BlockSpec((seq, cblk), lambda b, c: (b, c)),
                   pl.BlockSpec((ctx_len, cblk), lambda b, c: (b, c))],
        out_shape=[jax.ShapeDtypeStruct((n_lat, w), BF16), jax.ShapeDtypeStruct((n_ctx, w), BF16)],
        scratch_shapes=[pltpu.VMEM((seq + ctx_len, cblk), F32)] * 3,
        compiler_params=_params(("parallel", "parallel")),
        name="rglru",
    )(proj, proj, proj, proj, p['lru_conv_w'], p['lru_conv_b'].reshape(1, w), p['lru_w_rg'].astype(BF16), vec(p['lru_b_rg']),
      p['lru_w_ig'].astype(BF16), vec(p['lru_b_ig']), vec(p['lru_lam']))


def _mixers(proj, p, lam_init, geo, rows):
    n_lat, seq, bsz, n_ctx = geo
    w = BRANCH_W
    n_ctx_tok = n_ctx // bsz
    with_ctx = rows > n_lat

    def cols(lo, hi):
        return proj[n_lat:, lo:hi].reshape(bsz, n_ctx_tok, hi - lo), proj[:n_lat, lo:hi].reshape(bsz, seq, hi - lo)
    qkv_c, qkv_l = cols(w, 4 * w)
    z_c, z_l = cols(4 * w, 5 * w)
    a_c, a_l = cols(10 * w, 10 * w + 2 * DN_HEADS)
    b_c, b_l = cols(10 * w + 2 * DN_HEADS, 10 * w + 4 * DN_HEADS)
    dn_c, dn_l = _deltanet_mixer(qkv_c, a_c, b_c, z_c, qkv_l, a_l, b_l, z_l, p)
    lru_l, lru_c = _rglru_pallas(proj, p, geo)

    def join(lat, ctx_part):
        lat, ctx_part = lat.reshape(n_lat, w).astype(BF16), ctx_part.reshape(n_ctx, w).astype(BF16)
        return jnp.concatenate([lat, ctx_part], axis=0) if with_ctx else lat
    return [_s5_mixer_pallas(proj, p, geo, rows), join(dn_l, dn_c), _diff_attn_pallas(proj, p, lam_init, geo, rows), join(lru_l, lru_c)]


def kernel(x, c, ctx, c_ctx, ada_w, ada_b, mix_norm, ffn_norm, w_in, s5_lam_re, s5_lam_im, s5_log_dt, s5_b_re, s5_b_im, s5_c_re, s5_c_im, s5_d, s5_w_glu, dn_conv_w, dn_a_log, dn_dt_bias, dn_norm, da_lam_q1, da_lam_k1, da_lam_q2, da_lam_k2, da_subln, lru_conv_w, lru_conv_b, lru_w_rg, lru_b_rg, lru_w_ig, lru_b_ig, lru_lam, w_branch, w_out, router_w, router_b, moe_w_gu, moe_b_gu, moe_w_down, moe_b_down, final_norm):
    bsz, seq, d = x.shape
    n_ctx_tok = ctx.shape[1]
    depth = ada_w.shape[0]
    n_ctx = bsz * n_ctx_tok
    n_lat = bsz * seq
    n_tok = n_ctx + n_lat
    geo = (n_lat, seq, bsz, n_ctx)
    w = BRANCH_W
    mix_cols = 10 * w + 4 * DN_HEADS
    ab_lo = 5 * w

    xs = jnp.concatenate([x.reshape(n_lat, d), ctx.reshape(n_ctx, d)], axis=0)
    mods = _ada_table(c, c_ctx, ada_w, ada_b)

    for l in range(depth):
        with_ctx = l < depth - 1
        lam_init = 0.8 - 0.6 * math.exp(-0.3 * l)
        rows = n_tok if with_ctx else n_lat
        mod = mods[l]
        p = {'s5_lam_re': s5_lam_re[l], 's5_lam_im': s5_lam_im[l], 's5_log_dt': s5_log_dt[l], 's5_b_re': s5_b_re[l], 's5_b_im': s5_b_im[l], 's5_c_re': s5_c_re[l], 's5_c_im': s5_c_im[l], 's5_d': s5_d[l], 's5_w_glu': s5_w_glu[l], 'dn_conv_w': dn_conv_w[l], 'dn_a_log': dn_a_log[l], 'dn_dt_bias': dn_dt_bias[l], 'dn_norm': dn_norm[l], 'da_lam_q1': da_lam_q1[l], 'da_lam_k1': da_lam_k1[l], 'da_lam_q2': da_lam_q2[l], 'da_lam_k2': da_lam_k2[l], 'da_subln': da_subln[l], 'lru_conv_w': lru_conv_w[l], 'lru_conv_b': lru_conv_b[l], 'lru_w_rg': lru_w_rg[l], 'lru_b_rg': lru_b_rg[l], 'lru_w_ig': lru_w_ig[l], 'lru_b_ig': lru_b_ig[l], 'lru_lam': lru_lam[l]}
        wl = w_in[l]
        w_mix = jnp.concatenate([wl[:, :ab_lo], wl[:, ab_lo + 4 * DN_HEADS:mix_cols], wl[:, ab_lo:ab_lo + 4 * DN_HEADS],
                                 jnp.zeros((d, AB_COLS - 4 * DN_HEADS), F32)], axis=1).astype(BF16)
        w_gate = wl[:, mix_cols:].astype(BF16)
        wg = moe_w_gu[l][:, :, 0::2].astype(BF16)
        wu = moe_w_gu[l][:, :, 1::2].astype(BF16)
        bg = moe_b_gu[l][:, None, 0::2]
        bu = moe_b_gu[l][:, None, 1::2]
        wd = moe_w_down[l].astype(BF16)
        bd = moe_b_down[l][:, None, :]

        h = _norm_mod(xs, mix_norm[l], mod, geo)
        proj = _matmul(h, w_mix, F32)
        branches = _mixers(proj, p, lam_init, geo, rows)
        merged = _merge(h, branches, w_gate, w_branch[l].astype(BF16), rows)
        xs = _outproj_residual(merged, w_out[l].astype(BF16), xs, mod, geo, rows)
        final = l == depth - 1
        xs = _moe_layer(xs, ffn_norm[l], mod, router_w[l], router_b[l], wg, wu, bg, bu, wd, bd, final_norm, geo, rows, final)
    return xs.reshape(bsz, seq, d)
```

```python
import functools
import math

import jax
import jax.numpy as jnp
import numpy as np
from jax import lax
from jax.experimental import pallas as pl
from jax.experimental.pallas import tpu as pltpu

F32 = jnp.float32
BF16 = jnp.bfloat16

GRID_W = 64
N_BRANCH = 4
BRANCH_W = 1024
RMS_EPS = 1e-6
N_MOD = 6
ROPE_BASE = 10000.0
S5_GROUP = 16
S5_GROUPS = BRANCH_W // S5_GROUP
S5_STATE = 64
DN_HEADS = 8
DN_HEAD_DIM = BRANCH_W // DN_HEADS
DN_CHUNK = 64
DA_HEADS = 8
DA_HEAD_DIM = BRANCH_W // (2 * DA_HEADS)
DA_V_DIM = 2 * DA_HEAD_DIM
Q_BLOCK = 128
LRU_BLOCKS = 8
LRU_BLOCK_W = BRANCH_W // LRU_BLOCKS
LRU_C = 8.0
TOP_K = 4
SWIGLU_LIMIT = 7.0
SWIGLU_ALPHA = 1.702

MOD_ROWS = 8
LANES = 128
MOE_TILE = 256
AB_COLS = 128
VMEM_LIMIT = 56 << 20


def _params(sem, vmem=VMEM_LIMIT):
    return pltpu.CompilerParams(dimension_semantics=sem, vmem_limit_bytes=vmem)


def _tile(n, pref):
    t = min(n, pref)
    while n % t:
        t //= 2
    return t


def _ada_kernel(c_ref, w_ref, b_ref, o_ref):
    @pl.when(pl.program_id(2) == 0)
    def _():
        o_ref[0] = jnp.broadcast_to(b_ref[0], o_ref.shape[1:])
    c = c_ref[...]
    s = (c * jax.nn.sigmoid(c)).astype(BF16)
    o_ref[0] += jnp.dot(s, w_ref[0].astype(BF16), preferred_element_type=F32)


def _ada_table(c, c_ctx, ada_w, ada_b):
    depth, d, nm = ada_w.shape
    bsz = c.shape[0]
    cs = jnp.zeros((MOD_ROWS, d), F32).at[:bsz].set(c).at[bsz].set(c_ctx)
    tk, tn = _tile(d, 1024), _tile(nm, 2048)
    mod = pl.pallas_call(
        _ada_kernel,
        grid=(depth, nm // tn, d // tk),
        in_specs=[pl.BlockSpec((MOD_ROWS, tk), lambda l, n, k: (0, k)),
                  pl.BlockSpec((1, tk, tn), lambda l, n, k: (l, k, n)),
                  pl.BlockSpec((1, 1, tn), lambda l, n, k: (l, 0, n))],
        out_specs=pl.BlockSpec((1, MOD_ROWS, tn), lambda l, n, k: (l, 0, n)),
        out_shape=jax.ShapeDtypeStruct((depth, MOD_ROWS, nm), F32),
        compiler_params=_params(("parallel", "parallel", "arbitrary")),
        name="ada_table",
    )(cs, ada_w, ada_b.reshape(depth, 1, nm))
    return jnp.moveaxis(mod.reshape(depth, MOD_ROWS, N_MOD, d), 1, 2).reshape(depth, N_MOD * MOD_ROWS, 1, d)


def _mod_index(which, tm, geo):
    n_lat, seq, bsz, _ = geo

    def idx(i):
        r = i * tm
        return which * MOD_ROWS + jnp.where(r < n_lat, r // seq, bsz)
    return idx


def _row_tile(geo, pref):
    return _tile(math.gcd(geo[1], geo[3]), pref)


def _norm_mod_kernel(x_ref, g_ref, sh_ref, sc_ref, o_ref):
    x = x_ref[...]
    y = x * lax.rsqrt(jnp.mean(x * x, axis=-1, keepdims=True) + RMS_EPS) * g_ref[...]
    o_ref[...] = (y * (1.0 + sc_ref[0]) + sh_ref[0]).astype(o_ref.dtype)


def _norm_mod(x, gain, mod, geo):
    n, d = x.shape
    tm = _row_tile(geo,256)
    sh, sc = _mod_index(0, tm, geo), _mod_index(1, tm, geo)
    return pl.pallas_call(
        _norm_mod_kernel,
        grid=(n // tm,),
        in_specs=[pl.BlockSpec((tm, d), lambda i: (i, 0)),
                  pl.BlockSpec((1, d), lambda i: (0, 0)),
                  pl.BlockSpec((1, 1, d), lambda i: (sh(i), 0, 0)),
                  pl.BlockSpec((1, 1, d), lambda i: (sc(i), 0, 0))],
        out_specs=pl.BlockSpec((tm, d), lambda i: (i, 0)),
        out_shape=jax.ShapeDtypeStruct((n, d), BF16),
        compiler_params=_params(("parallel",)),
        name="norm_mod",
    )(x, gain.reshape(1, d), mod, mod)


def _mm_kernel(a_ref, b_ref, o_ref):
    o_ref[...] = jnp.dot(a_ref[...], b_ref[...], preferred_element_type=F32).astype(o_ref.dtype)


def _matmul(a, b, out_dtype, tm_pref=512, tn_pref=1152):
    m, k = a.shape
    n = b.shape[1]
    tm = _tile(m, tm_pref)
    tn = tn_pref if n % tn_pref == 0 else _tile(n, 1024)
    return pl.pallas_call(
        _mm_kernel,
        grid=(m // tm, n // tn),
        in_specs=[pl.BlockSpec((tm, k), lambda i, j: (i, 0)),
                  pl.BlockSpec((k, tn), lambda i, j: (0, j))],
        out_specs=pl.BlockSpec((tm, tn), lambda i, j: (i, j)),
        out_shape=jax.ShapeDtypeStruct((m, n), out_dtype),
        compiler_params=_params(("parallel", "parallel")),
        name="in_proj",
    )(a, b)


def _merge_kernel(h_ref, b0, b1, b2, b3, g0, g1, g2, g3, wb_ref, o_ref):
    h = h_ref[...]
    acc = None
    for k, (br, wg) in enumerate(((b0, g0), (b1, g1), (b2, g2), (b3, g3))):
        gate = jax.nn.sigmoid(jnp.dot(h, wg[...], preferred_element_type=F32))
        lifted = jnp.dot(br[...], wb_ref[k], preferred_element_type=F32)
        acc = gate * lifted if acc is None else acc + gate * lifted
    o_ref[...] = acc.astype(o_ref.dtype)


def _merge(h, branches, w_gate, w_branch, rows):
    d = h.shape[1]
    w = branches[0].shape[1]
    tm, tn = _tile(rows, 512), _tile(d, 256)
    nt = d // tn
    gate_specs = [pl.BlockSpec((d, tn), functools.partial(lambda i, j, k: (0, k * nt + j), k=k)) for k in range(N_BRANCH)]
    return pl.pallas_call(
        _merge_kernel,
        grid=(rows // tm, nt),
        in_specs=[pl.BlockSpec((tm, d), lambda i, j: (i,0))]
        + [pl.BlockSpec((tm, w), lambda i, j: (i,0))] * N_BRANCH
        + gate_specs
        + [pl.BlockSpec((N_BRANCH, w, tn), lambda i, j: (0, 0, j))],
        out_specs=pl.BlockSpec((tm, tn), lambda i, j: (i, j)),
        out_shape=jax.ShapeDtypeStruct((rows, d), BF16),
        compiler_params=_params(("parallel", "parallel")),
        name="merge",
    )(h, *branches, w_gate, w_gate, w_gate, w_gate, w_branch)


def _outproj_kernel(m_ref, w_ref, x_ref, g_ref, o_ref):
    o_ref[...] = x_ref[...] + g_ref[0] * jnp.dot(m_ref[...], w_ref[...], preferred_element_type=F32)


def _outproj_residual(merged, w_out, x, mod, geo, rows):
    n, d = x.shape
    tm, tn = _row_tile(geo,512), _tile(d, 1024)
    gi = _mod_index(2, tm, geo)
    return pl.pallas_call(
        _outproj_kernel,
        grid=(rows // tm, d // tn),
        in_specs=[pl.BlockSpec((tm, d), lambda i, j: (i, 0)),
                  pl.BlockSpec((d, tn), lambda i, j: (0, j)),
                  pl.BlockSpec((tm, tn), lambda i, j: (i,j)),
                  pl.BlockSpec((1, 1, tn), lambda i, j: (gi(i), 0, j))],
        out_specs=pl.BlockSpec((tm, tn), lambda i, j: (i, j)),
        out_shape=jax.ShapeDtypeStruct((rows, d), F32),
        compiler_params=_params(("parallel", "parallel")),
        name="out_proj",
    )(merged, w_out, x, mod)


def _router_kernel(n_experts, x_ref, g_ref, sh_ref, sc_ref, rw_ref, rb_ref, h_ref, ti_ref, tw_ref):
    x = x_ref[...]
    y = x * lax.rsqrt(jnp.mean(x * x, axis=-1, keepdims=True) + RMS_EPS) * g_ref[...]
    h = y * (1.0 + sc_ref[0]) + sh_ref[0]
    h_ref[...] = h
    logits = jnp.dot(h.astype(BF16), rw_ref[...], preferred_element_type=F32) + rb_ref[...]
    lane = lax.broadcasted_iota(jnp.int32, logits.shape, 1)
    logits = jnp.where(lane < n_experts, logits, -jnp.inf)
    idx_out = jnp.zeros(logits.shape, jnp.int32)
    val_out = jnp.zeros(logits.shape, F32)
    top = None
    denom = None
    for k in range(TOP_K):
        m = jnp.max(logits, axis=-1, keepdims=True)
        idx = jnp.min(jnp.where(logits == m, lane, LANES), axis=-1, keepdims=True)
        top = m if top is None else top
        e = jnp.exp(m - top)
        denom = e if denom is None else denom + e
        idx_out = jnp.where(lane == k, idx, idx_out)
        val_out = jnp.where(lane == k, e, val_out)
        logits = jnp.where(lane == idx, -jnp.inf, logits)
    ti_ref[...] = idx_out
    tw_ref[...] = val_out / denom


def _router(x, gain, mod, router_w, router_b, geo, rows):
    n, d = x.shape
    n_experts = router_w.shape[1]
    tm = _row_tile(geo,256)
    sh, sc = _mod_index(3, tm, geo), _mod_index(4, tm, geo)
    rw = jnp.zeros((d, LANES), BF16).at[:, :n_experts].set(router_w.astype(BF16))
    rb = jnp.zeros((1, LANES), F32).at[0, :n_experts].set(router_b)
    return pl.pallas_call(
        functools.partial(_router_kernel, n_experts),
        grid=(rows // tm,),
        in_specs=[pl.BlockSpec((tm, d), lambda i: (i, 0)),
                  pl.BlockSpec((1, d), lambda i: (0, 0)),
                  pl.BlockSpec((1, 1, d), lambda i: (sh(i), 0, 0)),
                  pl.BlockSpec((1, 1, d), lambda i: (sc(i), 0, 0)),
                  pl.BlockSpec((d, LANES), lambda i: (0, 0)),
                  pl.BlockSpec((1, LANES), lambda i: (0, 0))],
        out_specs=[pl.BlockSpec((tm, d), lambda i: (i, 0)),
                   pl.BlockSpec((tm, LANES), lambda i: (i, 0)),
                   pl.BlockSpec((tm, LANES), lambda i: (i, 0))],
        out_shape=[jax.ShapeDtypeStruct((rows, d), F32),
                   jax.ShapeDtypeStruct((rows, LANES), jnp.int32),
                   jax.ShapeDtypeStruct((rows, LANES), F32)],
        compiler_params=_params(("parallel",)),
        name="router",
    )(x, gain.reshape(1, d), mod, mod, rw, rb)


def _row_copy(src_hbm, row, dst, r, sem):
    return pltpu.make_async_copy(src_hbm.at[pl.ds(row, 1)], dst.at[pl.ds(r, 1)], sem)


def _expert_kernel(te_ref, used_ref, src_ref, h_hbm, wg_ref, wu_ref, bg_ref, bu_ref, wd_ref, bd_ref, o_ref, xbuf, sem):
    g = pl.program_id(0)
    n_used = used_ref[0]
    slot = g % 2

    def issue(tile, s):
        def body(r, carry):
            _row_copy(h_hbm, src_ref[tile * MOE_TILE + r], xbuf.at[s], r, sem.at[s]).start()
            return carry
        lax.fori_loop(0, MOE_TILE, body, 0, unroll=8)

    def wait(s):
        def body(r, carry):
            _row_copy(h_hbm, 0, xbuf.at[s], r, sem.at[s]).wait()
            return carry
        lax.fori_loop(0, MOE_TILE, body, 0, unroll=8)

    @pl.when((g == 0) & (n_used > 0))
    def _():
        issue(0, 0)

    @pl.when(g + 1 < n_used)
    def _():
        issue(g + 1, 1 - slot)

    @pl.when(g < n_used)
    def _():
        wait(slot)
        x = xbuf[slot].astype(BF16)
        gate = jnp.minimum(jnp.dot(x, wg_ref[0], preferred_element_type=F32) + bg_ref[0], SWIGLU_LIMIT)
        up = jnp.clip(jnp.dot(x, wu_ref[0], preferred_element_type=F32) + bu_ref[0], -SWIGLU_LIMIT, SWIGLU_LIMIT)
        act = (up + 1.0) * (gate * jax.nn.sigmoid(SWIGLU_ALPHA * gate))
        o_ref[...] = jnp.dot(act.astype(BF16), wd_ref[0], preferred_element_type=F32) + bd_ref[0]

    @pl.when(g >= n_used)
    def _():
        o_ref[...] = jnp.zeros_like(o_ref)


def _expert_ffn(h, tile_expert, n_used, row_src, wg, wu, bg, bu, wd, bd):
    d = h.shape[1]
    n_exp, _, ff = wg.shape
    n_tiles = tile_expert.shape[0]
    grid_spec = pltpu.PrefetchScalarGridSpec(
        num_scalar_prefetch=3,
        grid=(n_tiles,),
        in_specs=[pl.BlockSpec(memory_space=pl.ANY),
                  pl.BlockSpec((1, d, ff), lambda g, te, nu, rs: (te[g], 0, 0)),
                  pl.BlockSpec((1, d, ff), lambda g, te, nu, rs: (te[g], 0, 0)),
                  pl.BlockSpec((1, 1, ff), lambda g, te, nu, rs: (te[g], 0, 0)),
                  pl.BlockSpec((1, 1, ff), lambda g, te, nu, rs: (te[g], 0, 0)),
                  pl.BlockSpec((1, ff, d), lambda g, te, nu, rs: (te[g], 0, 0)),
                  pl.BlockSpec((1, 1, d), lambda g, te, nu, rs: (te[g], 0, 0))],
        out_specs=pl.BlockSpec((MOE_TILE, d), lambda g, te, nu, rs: (g, 0)),
        scratch_shapes=[pltpu.VMEM((2, MOE_TILE, d), F32), pltpu.SemaphoreType.DMA((2,))],
    )
    return pl.pallas_call(
        _expert_kernel,
        grid_spec=grid_spec,
        out_shape=jax.ShapeDtypeStruct((n_tiles * MOE_TILE, d), F32),
        compiler_params=_params(("arbitrary",)),
        name="expert_ffn",
    )(tile_expert, n_used, row_src, h, wg, wu, bg, bu, wd, bd)


def _combine_kernel(tm, final, pos_ref, x_ref, tw_ref, g_ref, fn_ref, ys_hbm, o_ref, ybuf, sem):
    i = pl.program_id(0)
    n_steps = pl.num_programs(0)
    slot = i % 2

    def issue(step, s):
        def body(r, carry):
            for k in range(TOP_K):
                _row_copy(ys_hbm, pos_ref[(step * tm + r) * TOP_K + k], ybuf.at[s, k], r, sem.at[s]).start()
            return carry
        lax.fori_loop(0, tm, body, 0, unroll=4)

    def wait(s):
        def body(r, carry):
            for k in range(TOP_K):
                _row_copy(ys_hbm, 0, ybuf.at[s, k], r, sem.at[s]).wait()
            return carry
        lax.fori_loop(0, tm, body, 0, unroll=4)

    @pl.when(i == 0)
    def _():
        issue(0, 0)

    @pl.when(i + 1 < n_steps)
    def _():
        issue(i + 1, 1 - slot)

    wait(slot)
    tw = tw_ref[...]
    acc = tw[:, 0:1] * ybuf[slot, 0]
    for k in range(1, TOP_K):
        acc = acc + tw[:, k:k + 1] * ybuf[slot, k]
    y = x_ref[...] + g_ref[0] * acc
    if final:
        y = y * lax.rsqrt(jnp.mean(y * y, axis=-1, keepdims=True) + RMS_EPS) * fn_ref[...]
    o_ref[...] = y


def _combine(x, pos, tw, ys, mod, final_gain, geo, rows, final):
    n, d = x.shape
    tm = _row_tile(geo,128)
    gi = _mod_index(5, tm, geo)
    grid_spec = pltpu.PrefetchScalarGridSpec(
        num_scalar_prefetch=1,
        grid=(rows // tm,),
        in_specs=[pl.BlockSpec((tm, d), lambda i, p: (i, 0)),
                  pl.BlockSpec((tm, LANES), lambda i, p: (i, 0)),
                  pl.BlockSpec((1, 1, d), lambda i, p: (gi(i), 0, 0)),
                  pl.BlockSpec((1, d), lambda i, p: (0, 0)),
                  pl.BlockSpec(memory_space=pl.ANY)],
        out_specs=pl.BlockSpec((tm, d), lambda i, p: (i, 0)),
        scratch_shapes=[pltpu.VMEM((2, TOP_K, tm, d), F32), pltpu.SemaphoreType.DMA((2,))],
    )
    return pl.pallas_call(
        functools.partial(_combine_kernel, tm, final),
        grid_spec=grid_spec,
        out_shape=jax.ShapeDtypeStruct((rows, d), F32),
        compiler_params=_params(("arbitrary",)),
        name="moe_combine",
    )(pos, x, tw, mod, final_gain.reshape(1, d), ys)


def _moe_layer(x, gain, mod, router_w, router_b, wg, wu, bg, bu, wd, bd, final_gain, geo, rows, final):
    n_exp = wg.shape[0]
    h, top_i, top_w = _router(x, gain, mod, router_w, router_b, geo, rows)
    e_flat = top_i[:, :TOP_K].reshape(-1)
    onehot = (e_flat[:, None] == jnp.arange(n_exp, dtype=jnp.int32)[None, :]).astype(jnp.int32)
    rank = jnp.sum((jnp.cumsum(onehot, axis=0) - onehot) * onehot, axis=1)
    counts = jnp.sum(onehot, axis=0)
    padded = ((counts + MOE_TILE - 1) // MOE_TILE) * MOE_TILE
    ends = jnp.cumsum(padded)
    pos = ((ends - padded)[e_flat] + rank).astype(jnp.int32)
    n_tiles = (rows * TOP_K + n_exp * (MOE_TILE - 1)) // MOE_TILE + 1
    tok = jnp.arange(rows * TOP_K, dtype=jnp.int32) // TOP_K
    row_src = jnp.zeros((n_tiles * MOE_TILE,), jnp.int32).at[pos].set(tok, unique_indices=True)
    tile_start = jnp.arange(n_tiles, dtype=jnp.int32) * MOE_TILE
    tile_expert = jnp.minimum(jnp.searchsorted(ends, tile_start, side="right"), n_exp - 1).astype(jnp.int32)
    n_used = (ends[-1] // MOE_TILE).astype(jnp.int32).reshape(1)
    ys = _expert_ffn(h, tile_expert, n_used, row_src, wg, wu, bg, bu, wd, bd)
    return _combine(x, pos, top_w, ys, mod, final_gain, geo, rows, final)


def _flip(t, rev):
    return jnp.flip(t, axis=1) if rev else t


def _linear_combine(e1, e2):
    a1, b1 = e1
    a2, b2 = e2
    return a1 * a2, a2 * b1 + b2


def _dwconv(x, w, b=None):
    k = w.shape[0]
    y = lax.conv_general_dilated(x, w[:, None, :].astype(x.dtype), (1,), [(k // 2, k - 1 - k // 2)], dimension_numbers=('NWC', 'WIO', 'NWC'), feature_group_count=x.shape[-1])
    return y if b is None else y + b.astype(x.dtype)


def _l2norm(t):
    return t * lax.rsqrt(jnp.sum(t * t, axis=-1, keepdims=True) + 1e-6)


def _rmsnorm(x, g):
    xf = x.astype(F32)
    y = xf * lax.rsqrt(jnp.mean(xf * xf, axis=-1, keepdims=True) + RMS_EPS)
    return (y * g.astype(F32)).astype(x.dtype)


def _axial_rope(n_tok, rot_dim):
    n_rows = n_tok // GRID_W
    rows = jnp.repeat(jnp.arange(n_rows), GRID_W).astype(F32)
    cols = jnp.tile(jnp.arange(GRID_W), n_rows).astype(F32)
    n_freq = rot_dim // 4
    inv = ROPE_BASE ** (-jnp.arange(n_freq, dtype=F32) / n_freq)
    ang = jnp.concatenate([rows[:, None] * inv, cols[:, None] * inv], axis=-1)
    return jnp.cos(ang), jnp.sin(ang)


def _rope(x, cos, sin):
    half = x.shape[-1] // 2
    shp = (1, cos.shape[0]) + (1,) * (x.ndim - 3) + (half,)
    c, s = cos.reshape(shp), sin.reshape(shp)
    x1, x2 = x[..., :half], x[..., half:]
    return jnp.concatenate([x1 * c - x2 * s, x1 * s + x2 * c], axis=-1).astype(x.dtype)


def _s5_scan(u, lam_bar, b_bar, c_mat, h0):
    bu = jnp.einsum('blgc,gpc->blgp', u.astype(jnp.complex64), b_bar)
    bu = bu.at[:, 0].add(lam_bar * h0)
    a = jnp.broadcast_to(lam_bar, bu.shape)
    _, h = lax.associative_scan(_linear_combine, (a, bu), axis=1)
    y = jnp.einsum('blgp,gcp->blgc', h, c_mat).real
    return y, h[:, -1]


def _s5_mixer(u_ctx, u_lat, p):
    def groups(u):
        return u.astype(F32).reshape(u.shape[0], u.shape[1], S5_GROUPS, S5_GROUP)
    uc, ul = groups(u_ctx), groups(u_lat)
    y_ctx = jnp.zeros_like(uc)
    y_lat = jnp.zeros_like(ul)
    for d, rev in enumerate((False, True)):
        lam = lax.complex(p['s5_lam_re'][d], p['s5_lam_im'][d])
        dt = jnp.exp(p['s5_log_dt'][d])[:, None]
        lam_bar = jnp.exp(lam * dt)
        b_mat = lax.complex(p['s5_b_re'][d], p['s5_b_im'][d])
        b_bar = ((lam_bar - 1.0) / lam)[..., None] * b_mat
        c_mat = lax.complex(p['s5_c_re'][d], p['s5_c_im'][d])
        h0 = jnp.zeros((uc.shape[0], S5_GROUPS, S5_STATE), jnp.complex64)
        yc, h_ctx = _s5_scan(_flip(uc, rev), lam_bar, b_bar, c_mat, h0)
        yl, _ = _s5_scan(_flip(ul, rev), lam_bar, b_bar, c_mat, h_ctx)
        y_ctx = y_ctx + _flip(yc, rev)
        y_lat = y_lat + _flip(yl, rev)
    d_skip = p['s5_d'].reshape(S5_GROUPS, S5_GROUP)
    w_glu = p['s5_w_glu']

    def readout(y, u):
        z = jax.nn.gelu((y + d_skip * u).reshape(u.shape[0], u.shape[1], BRANCH_W))
        return z * jax.nn.sigmoid(z @ w_glu)
    return readout(y_ctx, uc), readout(y_lat, ul)


def _gated_delta_rule(q, k, v, g, beta, s0):
    bsz, n_tok, n_h, _ = q.shape
    dv = v.shape[-1]
    n_chunk = n_tok // DN_CHUNK

    def chunks(t):
        return jnp.moveaxis(t.reshape(bsz, n_chunk, DN_CHUNK, n_h, -1), (1, 3), (0, 2))
    qc, kc, vc = chunks(q), chunks(k), chunks(v)
    gcum = jnp.cumsum(chunks(g[..., None])[..., 0], axis=-1)
    bc = chunks(beta[..., None])[..., 0]
    causal = jnp.tril(jnp.ones((DN_CHUNK, DN_CHUNK), bool))
    strict = jnp.tril(jnp.ones((DN_CHUNK, DN_CHUNK), bool), -1)
    decay = jnp.exp(jnp.where(causal, gcum[..., :, None] - gcum[..., None, :], -jnp.inf))
    kb = kc * bc[..., None]
    a_low = jnp.where(strict, jnp.einsum('nbhid,nbhjd->nbhij', kb, kc) * decay, 0.0)
    eye = jnp.eye(DN_CHUNK, dtype=F32)
    t_inv = lax.linalg.triangular_solve(a_low + eye, jnp.broadcast_to(eye, a_low.shape), left_side=True, lower=True, unit_diagonal=True)
    u = t_inv @ (vc * bc[..., None])
    w = t_inv @ (kb * jnp.exp(gcum)[..., None])
    qk = jnp.einsum('nbhid,nbhjd->nbhij', qc, kc) * decay

    def step(s, inp):
        q_i, k_i, u_i, w_i, g_i, qk_i = inp
        v_new = u_i - w_i @ s
        o = (q_i * jnp.exp(g_i)[..., None]) @ s + qk_i @ v_new
        g_last = g_i[..., -1:]
        k_dec = k_i * jnp.exp(g_last - g_i)[..., None]
        s = s * jnp.exp(g_last)[..., None] + jnp.einsum('bhcd,bhce->bhde', k_dec, v_new)
        return s, o
    s_fin, o = lax.scan(step, s0, (qc, kc, u, w, gcum, qk))
    o = jnp.moveaxis(o, (0, 2), (1, 3)).reshape(bsz, n_tok, n_h, dv)
    return o, s_fin


def _dn_prep(qkv, a, b, p):
    bsz, n = qkv.shape[:2]
    qkv = jax.nn.silu(_dwconv(qkv, p['dn_conv_w'])).reshape(bsz, n, 3, DN_HEADS, DN_HEAD_DIM)
    q = _l2norm(qkv[:, :, 0]) * DN_HEAD_DIM ** -0.5
    k = _l2norm(qkv[:, :, 1])
    v = qkv[:, :, 2]
    a = a.reshape(bsz, n, 2, DN_HEADS)
    g = -jnp.exp(p['dn_a_log']) * jax.nn.softplus(a + p['dn_dt_bias'])
    beta = jax.nn.sigmoid(b.reshape(bsz, n, 2, DN_HEADS))
    return q, k, v, g, beta


def _deltanet_mixer(qkv_c, a_c, b_c, z_c, qkv_l, a_l, b_l, z_l, p):
    ctx_in = _dn_prep(qkv_c, a_c, b_c, p)
    lat_in = _dn_prep(qkv_l, a_l, b_l, p)
    bsz = qkv_l.shape[0]
    o_ctx = jnp.zeros_like(ctx_in[2])
    o_lat = jnp.zeros_like(lat_in[2])
    for d, rev in enumerate((False, True)):
        def dir_inputs(t):
            q, k, v, g, beta = t
            return [_flip(z, rev) for z in (q, k, v, g[:, :, d], beta[:, :, d])]
        s0 = jnp.zeros((bsz, DN_HEADS, DN_HEAD_DIM, DN_HEAD_DIM), F32)
        oc, s_ctx = _gated_delta_rule(*dir_inputs(ctx_in), s0)
        ol, _ = _gated_delta_rule(*dir_inputs(lat_in), s_ctx)
        o_ctx = o_ctx + _flip(oc, rev)
        o_lat = o_lat + _flip(ol, rev)

    def readout(o, z):
        y = _rmsnorm(o, p['dn_norm']) * jax.nn.silu(z.reshape(o.shape))
        return y.reshape(o.shape[0], o.shape[1], BRANCH_W)
    return readout(o_ctx, z_c), readout(o_lat, z_l)


def _diff_attn_mixer(q_c, k_c, v_c, q_l, k_l, v_l, p, lam_init):
    bsz, n_lat = q_l.shape[:2]

    def qk_heads(t):
        return t.reshape(t.shape[0], t.shape[1], DA_HEADS, 2, DA_HEAD_DIM)

    def v_heads(t):
        return t.reshape(t.shape[0], t.shape[1], DA_HEADS, DA_V_DIM)
    cos, sin = _axial_rope(n_lat, DA_HEAD_DIM)
    qc, kc, vc = qk_heads(q_c), qk_heads(k_c), v_heads(v_c)
    ql, kl, vl = _rope(qk_heads(q_l), cos, sin), _rope(qk_heads(k_l), cos, sin), v_heads(v_l)
    lam = (jnp.exp(jnp.sum(p['da_lam_q1'] * p['da_lam_k1'])) - jnp.exp(jnp.sum(p['da_lam_q2'] * p['da_lam_k2'])) + lam_init)
    scale = DA_HEAD_DIM ** -0.5

    def attend(q, k, v):
        s = jnp.einsum('bqhmd,bkhmd->bhmqk', q, k) * scale
        pr = jax.nn.softmax(s, axis=-1)
        w = pr[:, :, 0] - lam * pr[:, :, 1]
        return jnp.einsum('bhqk,bkhe->bqhe', w, v)
    o_ctx = attend(qc, kc, vc)
    k_all = jnp.concatenate([kl, kc], axis=1)
    v_all = jnp.concatenate([vl, vc], axis=1)
    n_blk = n_lat // Q_BLOCK
    q_blocks = jnp.moveaxis(ql.reshape(bsz, n_blk, Q_BLOCK, DA_HEADS, 2, DA_HEAD_DIM), 1, 0)
    o_lat = lax.map(lambda qb: attend(qb, k_all, v_all), q_blocks)
    o_lat = jnp.moveaxis(o_lat, 0, 1).reshape(bsz, n_lat, DA_HEADS, DA_V_DIM)

    def readout(o):
        y = _rmsnorm(o, p['da_subln']) * (1.0 - lam_init)
        return y.reshape(o.shape[0], o.shape[1], BRANCH_W)
    return readout(o_ctx), readout(o_lat)


def _rglru_scan(x, w_r, b_r, w_i, b_i, lam, h0):
    bsz, n, _ = x.shape
    xb = x.reshape(bsz, n, LRU_BLOCKS, LRU_BLOCK_W)
    r = jax.nn.sigmoid(jnp.einsum('blhi,hij->blhj', xb, w_r).reshape(bsz, n, BRANCH_W) + b_r)
    i = jax.nn.sigmoid(jnp.einsum('blhi,hij->blhj', xb, w_i).reshape(bsz, n, BRANCH_W) + b_i)
    log_a = -LRU_C * r * jax.nn.softplus(-lam)
    a = jnp.exp(log_a)
    b = jnp.sqrt(-jnp.expm1(2.0 * log_a)) * (i * x)
    b = b.at[:, 0].add(a[:, 0] * h0)
    _, h = lax.associative_scan(_linear_combine, (a, b), axis=1)
    return h, h[:, -1]


def _rglru_mixer(x_c, g_c, x_l, g_l, p):
    xc = _dwconv(x_c, p['lru_conv_w'], p['lru_conv_b'])
    xl = _dwconv(x_l, p['lru_conv_w'], p['lru_conv_b'])
    h_ctx = jnp.zeros_like(xc)
    h_lat = jnp.zeros_like(xl)
    for d, rev in enumerate((False, True)):
        prm = tuple(p[name][d] for name in ('lru_w_rg', 'lru_b_rg', 'lru_w_ig', 'lru_b_ig', 'lru_lam'))
        h0 = jnp.zeros((xc.shape[0], BRANCH_W), F32)
        hc, h_fin = _rglru_scan(_flip(xc, rev), *prm, h0)
        hl, _ = _rglru_scan(_flip(xl, rev), *prm, h_fin)
        h_ctx = h_ctx + _flip(hc, rev)
        h_lat = h_lat + _flip(hl, rev)
    return h_ctx * jax.nn.gelu(g_c), h_lat * jax.nn.gelu(g_l)


S5_CHUNK = 16
S5_ROW = S5_CHUNK * S5_GROUP
S5_GBLK = 8


def _s5_operators(p, n_chunks):
    hi = lax.Precision.HIGHEST
    n_steps = max(1, (n_chunks - 1).bit_length())
    ci = jnp.arange(S5_CHUNK)
    t_ops, in_ops, out_ops, scan_ops = [], [], [], []
    for d in range(2):
        lam = lax.complex(p['s5_lam_re'][d], p['s5_lam_im'][d])
        dt = jnp.exp(p['s5_log_dt'][d])[:, None]
        lam_bar = jnp.exp(lam * dt)
        b_bar = ((lam_bar - 1.0) / lam)[..., None] * lax.complex(p['s5_b_re'][d], p['s5_b_im'][d])
        c_mat = lax.complex(p['s5_c_re'][d], p['s5_c_im'][d])
        pw = [jnp.ones_like(lam_bar)]
        for _ in range(S5_CHUNK):
            pw.append(pw[-1] * lam_bar)
        pw = jnp.stack(pw)
        kern = jnp.einsum('gop,mgp,gpi->mgoi', c_mat, pw[:S5_CHUNK], b_bar, precision=hi).real
        lag = (ci[None, :] - ci[:, None]) if d == 0 else (ci[:, None] - ci[None, :])
        t_op = jnp.where((lag >= 0)[:, :, None, None, None], kern[jnp.clip(lag, 0, S5_CHUNK - 1)], 0.0)
        t_ops.append(jnp.transpose(t_op, (2, 0, 4, 1, 3)).reshape(S5_GROUPS, S5_ROW, S5_ROW))
        e_in = (S5_CHUNK - 1 - ci) if d == 0 else ci
        m_in = pw[e_in][:, :, :, None] * b_bar[None]
        m_in = jnp.transpose(m_in, (1, 0, 3, 2)).reshape(S5_GROUPS, S5_ROW, S5_STATE)
        in_ops.append(jnp.concatenate([m_in.real, m_in.imag], axis=-1))
        e_out = (ci + 1) if d == 0 else (S5_CHUNK - ci)
        m_out = c_mat[None] * pw[e_out][:, :, None, :]
        m_out = jnp.transpose(m_out, (1, 3, 0, 2)).reshape(S5_GROUPS, S5_STATE, S5_ROW)
        out_ops.append(jnp.concatenate([m_out.real, -m_out.imag], axis=1))
        a = pw[S5_CHUNK]
        mults = []
        for _ in range(n_steps):
            mults.append(a)
            a = a * a
        mults = jnp.stack(mults, axis=1)
        scan_ops.append(jnp.concatenate([jnp.concatenate([mults.real, mults.real], -1), jnp.concatenate([-mults.imag, mults.imag], -1)], axis=1))
    return (jnp.stack(t_ops).astype(BF16), jnp.stack(in_ops).astype(BF16), jnp.stack(out_ops).astype(BF16), jnp.stack(scan_ops))


def _s5_kernel(nb, n_steps, ul_ref, uc_ref, t_ref, in_ref, out_ref, sc_ref, yl_ref, yc_ref):
    n_l, n_c = ul_ref.shape[1], uc_ref.shape[1]

    def shift(x, rows, reverse):
        n = x.shape[0]
        row = lax.broadcasted_iota(jnp.int32, x.shape, 0)
        if reverse:
            return jnp.where(row < n - rows, pltpu.roll(x, n - rows, axis=0), 0.0)
        return jnp.where(row >= rows, pltpu.roll(x, rows, axis=0), 0.0)

    def group(g, carry):
        ul, uc = ul_ref[g], uc_ref[g]
        ys = []
        for d in range(2):
            rev = d == 1
            u = jnp.concatenate([ul, uc], axis=0) if rev else jnp.concatenate([uc, ul], axis=0)
            y = jnp.dot(u, t_ref[d, g], preferred_element_type=F32)
            x = jnp.dot(u, in_ref[d, g], preferred_element_type=F32)
            sc = sc_ref[d, g]
            for k in range(n_steps):
                if nb * 2 ** k >= x.shape[0]:
                    break
                xs = shift(x, nb * 2 ** k, rev)
                x = x + sc[k:k + 1] * xs + sc[n_steps + k:n_steps + k + 1] * pltpu.roll(xs, S5_STATE, axis=1)
            h_in = shift(x, nb, rev)
            ys.append(y + jnp.dot(h_in.astype(BF16), out_ref[d, g], preferred_element_type=F32))
        yl_ref[g] = ys[0][n_c:] + ys[1][:n_l]
        yc_ref[g] = ys[0][:n_c] + ys[1][n_l:]
        return carry
    lax.fori_loop(0, ul_ref.shape[0], group, 0)


def _s5_readout_kernel(y_ref, u_ref, d_ref, w_ref, o_ref):
    z = jax.nn.gelu(y_ref[...] + d_ref[...] * u_ref[...])
    o_ref[...] = (z * jax.nn.sigmoid(jnp.dot(z.astype(BF16), w_ref[...], preferred_element_type=F32))).astype(o_ref.dtype)


def _s5_mixer_pallas(proj, p, geo, rows):
    n_lat, seq, bsz, n_ctx = geo
    w = BRANCH_W
    nl, nc = seq // S5_CHUNK, (n_ctx // bsz) // S5_CHUNK
    n_steps = max(1, (nl + nc - 1).bit_length())
    t_op, in_op, out_op, scan_op = _s5_operators(p, nl + nc)

    def to_rows(u, n):
        u = u.reshape(bsz, n, S5_CHUNK, S5_GROUPS, S5_GROUP)
        return jnp.transpose(u, (3, 1, 0, 2, 4)).reshape(S5_GROUPS, n * bsz, S5_ROW)

    def from_rows(y, n):
        y = y.reshape(S5_GROUPS, n, bsz, S5_CHUNK, S5_GROUP)
        return jnp.transpose(y, (2, 1, 3, 0, 4)).reshape(bsz * n * S5_CHUNK, w)
    u = proj[:, :w]
    ul, uc = to_rows(u[:n_lat].astype(BF16), nl), to_rows(u[n_lat:].astype(BF16), nc)
    gb = S5_GBLK
    yl, yc = pl.pallas_call(
        functools.partial(_s5_kernel, bsz, n_steps),
        grid=(S5_GROUPS // gb,),
        in_specs=[pl.BlockSpec((gb, nl * bsz, S5_ROW), lambda i: (i, 0, 0)),
                  pl.BlockSpec((gb, nc * bsz, S5_ROW), lambda i: (i, 0, 0)),
                  pl.BlockSpec((2, gb, S5_ROW, S5_ROW), lambda i: (0, i, 0, 0)),
                  pl.BlockSpec((2, gb, S5_ROW, 2 * S5_STATE), lambda i: (0, i, 0, 0)),
                  pl.BlockSpec((2, gb, 2 * S5_STATE, S5_ROW), lambda i: (0, i, 0, 0)),
                  pl.BlockSpec((2, gb, 2 * n_steps, 2 * S5_STATE), lambda i: (0, i, 0, 0))],
        out_specs=[pl.BlockSpec((gb, nl * bsz, S5_ROW), lambda i: (i, 0, 0)),
                   pl.BlockSpec((gb, nc * bsz, S5_ROW), lambda i: (i, 0, 0))],
        out_shape=[jax.ShapeDtypeStruct((S5_GROUPS, nl * bsz, S5_ROW), F32),
                   jax.ShapeDtypeStruct((S5_GROUPS, nc * bsz, S5_ROW), F32)],
        compiler_params=_params(("parallel",)),
        name="s5_scan",
    )(ul, uc, t_op, in_op, out_op, scan_op)
    y = from_rows(yl, nl) if rows == n_lat else jnp.concatenate([from_rows(yl, nl), from_rows(yc, nc)], axis=0)
    tm = _row_tile(geo, 256)
    return pl.pallas_call(
        _s5_readout_kernel,
        grid=(rows // tm,),
        in_specs=[pl.BlockSpec((tm, w), lambda i: (i, 0)),
                  pl.BlockSpec((tm, w), lambda i: (i, 0)),
                  pl.BlockSpec((1, w), lambda i: (0, 0)),
                  pl.BlockSpec((w, w), lambda i: (0, 0))],
        out_specs=pl.BlockSpec((tm, w), lambda i: (i, 0)),
        out_shape=jax.ShapeDtypeStruct((rows, w), BF16),
        compiler_params=_params(("parallel",)),
        name="s5_readout",
    )(y, proj, p['s5_d'].reshape(1, w), p['s5_w_glu'].astype(BF16))


def _rope_tables(seq, blk):
    t = jnp.arange(seq)
    n_freq = DA_HEAD_DIM // 4
    inv = ROPE_BASE ** (-jnp.arange(n_freq, dtype=F32) / n_freq)
    ang = jnp.concatenate([(t // GRID_W).astype(F32)[:, None] * inv, (t % GRID_W).astype(F32)[:, None] * inv], axis=-1)
    cos = jnp.concatenate([jnp.cos(ang), jnp.cos(ang)], axis=-1)
    sin = jnp.concatenate([-jnp.sin(ang), jnp.sin(ang)], axis=-1)
    reps = BRANCH_W // DA_HEAD_DIM
    cos = jnp.concatenate([jnp.tile(cos, (1, reps)), jnp.ones((blk, BRANCH_W), F32)], axis=0)
    sin = jnp.concatenate([jnp.tile(sin, (1, reps)), jnp.zeros((blk, BRANCH_W), F32)], axis=0)
    return cos, sin


def _da_prep_kernel(q_ref, k_ref, v_ref, cos_ref, sin_ref, qo_ref, ko_ref, vo_ref):
    cos, sin = cos_ref[...], sin_ref[...]
    w = cos.shape[1]
    half = DA_HEAD_DIM // 2
    first = (lax.broadcasted_iota(jnp.int32, cos.shape, 1) % DA_HEAD_DIM) < half

    def rope(x):
        partner = jnp.where(first, pltpu.roll(x, w - half, axis=1), pltpu.roll(x, half, axis=1))
        return x * cos + partner * sin
    qo_ref[...] = (rope(q_ref[...]) * DA_HEAD_DIM ** -0.5).astype(qo_ref.dtype)
    ko_ref[...] = rope(k_ref[...]).astype(ko_ref.dtype)
    vo_ref[...] = v_ref[...].astype(vo_ref.dtype)


def _da_kernel(n_qb, q_ref, kl_ref, kc_ref, vl_ref, vc_ref, prm_ref, o_ref):
    lam = prm_ref[0:1, :]
    gain = prm_ref[1:2, :]
    nt = (((1,), (1,)), ((), ()))

    def attend(keys):
        q = q_ref[...]
        outs = []
        for m in range(2):
            lo, hi = m * DA_HEAD_DIM, (m + 1) * DA_HEAD_DIM
            s = [lax.dot_general(q[:, lo:hi], k_ref[:, lo:hi], nt, preferred_element_type=F32) for k_ref, _ in keys]
            mx = functools.reduce(jnp.maximum, [jnp.max(t, axis=-1, keepdims=True) for t in s])
            pr = [jnp.exp(t - mx) for t in s]
            den = functools.reduce(jnp.add, [jnp.sum(t, axis=-1, keepdims=True) for t in pr])
            num = functools.reduce(jnp.add, [jnp.dot(t.astype(BF16), v_ref[...], preferred_element_type=F32) for t, (_, v_ref) in zip(pr, keys)])
            outs.append(num / den)
        o = outs[0] - lam * outs[1]
        y = o * lax.rsqrt(jnp.mean(o * o, axis=-1, keepdims=True) + RMS_EPS) * gain
        o_ref[...] = y.astype(o_ref.dtype)

    @pl.when(pl.program_id(2) < n_qb)
    def _():
        attend([(kl_ref, vl_ref), (kc_ref, vc_ref)])

    @pl.when(pl.program_id(2) >= n_qb)
    def _():
        attend([(kc_ref, vc_ref)])


def _diff_attn_pallas(proj, p, lam_init, geo, rows):
    n_lat, seq, bsz, n_ctx = geo
    w = BRANCH_W
    n = proj.shape[0]
    tq = n_ctx // bsz
    n_qb = seq // tq
    cos, sin = _rope_tables(seq, tq)
    tab = lambda i: (jnp.where(i * tq < n_lat, i % n_qb, n_qb), 0)
    qr, kr, vb = pl.pallas_call(
        _da_prep_kernel,
        grid=(n // tq,),
        in_specs=[pl.BlockSpec((tq, w), lambda i: (i, 5)), pl.BlockSpec((tq, w), lambda i: (i, 6)), pl.BlockSpec((tq, w), lambda i: (i, 7)),
                  pl.BlockSpec((tq, w), tab), pl.BlockSpec((tq, w), tab)],
        out_specs=[pl.BlockSpec((tq, w), lambda i: (i, 0))] * 3,
        out_shape=[jax.ShapeDtypeStruct((n, w), BF16)] * 3,
        compiler_params=_params(("parallel",)),
        name="da_prep",
    )(proj, proj, proj, cos, sin)
    lam = jnp.exp(jnp.sum(p['da_lam_q1'] * p['da_lam_k1'])) - jnp.exp(jnp.sum(p['da_lam_q2'] * p['da_lam_k2'])) + lam_init
    prm = jnp.zeros((8, DA_V_DIM), F32).at[0].set(lam).at[1].set(p['da_subln'] * (1.0 - lam_init))
    with_ctx = rows > n_lat
    ctx_blk = n_lat // tq
    qrow = lambda b, h, i: (jnp.where(i < n_qb, b * n_qb + i, ctx_blk + b), h)
    return pl.pallas_call(
        functools.partial(_da_kernel, n_qb),
        grid=(bsz, DA_HEADS, n_qb + (1 if with_ctx else 0)),
        in_specs=[pl.BlockSpec((tq, DA_V_DIM), qrow),
                  pl.BlockSpec((seq, DA_V_DIM), lambda b, h, i: (b, h)),
                  pl.BlockSpec((tq, DA_V_DIM), lambda b, h, i: (ctx_blk + b, h)),
                  pl.BlockSpec((seq, DA_V_DIM), lambda b, h, i: (b, h)),
                  pl.BlockSpec((tq, DA_V_DIM), lambda b, h, i: (ctx_blk + b, h)),
                  pl.BlockSpec((8, DA_V_DIM), lambda b, h, i: (0, 0))],
        out_specs=pl.BlockSpec((tq, DA_V_DIM), qrow),
        out_shape=jax.ShapeDtypeStruct((rows, w), BF16),
        compiler_params=_params(("parallel", "parallel", "arbitrary")),
        name="diff_attn",
    )(qr, kr, kr, vb, vb, prm)


LRU_CBLK = 256
SUBLANES = 8


def _lru_kernel(xl_ref, xc_ref, gl_ref, gc_ref, cw_ref, cb_ref, wr_ref, br_ref, wi_ref, bi_ref, lam_ref, ol_ref, oc_ref, a_sc, b_sc, h_sc):
    n_l, n_c = xl_ref.shape[0], xc_ref.shape[0]
    n_all = n_l + n_c
    cw, cb = cw_ref[...], cb_ref[...]

    def conv(x):
        n = x.shape[0]
        row = lax.broadcasted_iota(jnp.int32, x.shape, 0)
        xm2 = jnp.where(row >= 2, pltpu.roll(x, 2, axis=0), 0.0)
        xm1 = jnp.where(row >= 1, pltpu.roll(x, 1, axis=0), 0.0)
        xp1 = jnp.where(row < n - 1, pltpu.roll(x, n - 1, axis=0), 0.0)
        return cw[0:1] * xm2 + cw[1:2] * xm1 + cw[2:3] * x + cw[3:4] * xp1 + cb
    xc = jnp.concatenate([conv(xc_ref[...]), conv(xl_ref[...])], axis=0)
    xb = xc.astype(BF16)
    sub = lax.broadcasted_iota(jnp.int32, xc.shape, 0) % SUBLANES
    n_blk = xc.shape[1] // LRU_BLOCK_W

    def gate(w_ref, b_ref, d):
        cols = [jnp.dot(xb[:, k * LRU_BLOCK_W:(k + 1) * LRU_BLOCK_W], w_ref[d, k], preferred_element_type=F32) for k in range(n_blk)]
        return jax.nn.sigmoid(jnp.concatenate(cols, axis=1) + b_ref[d])

    for d in range(2):
        rev = d == 1
        r, i = gate(wr_ref, br_ref, d), gate(wi_ref, bi_ref, d)
        log_a = -LRU_C * r * jax.nn.softplus(-lam_ref[d])
        a = jnp.exp(log_a)
        b = jnp.sqrt(1.0 - a * a) * (i * xc)
        for s in (1, 2, 4):
            if rev:
                keep = sub < SUBLANES - s
                a_n = jnp.where(keep, pltpu.roll(a, n_all - s, axis=0), 1.0)
                b_n = jnp.where(keep, pltpu.roll(b, n_all - s, axis=0), 0.0)
            else:
                keep = sub >= s
                a_n = jnp.where(keep, pltpu.roll(a, s, axis=0), 1.0)
                b_n = jnp.where(keep, pltpu.roll(b, s, axis=0), 0.0)
            b = b + a * b_n
            a = a * a_n
        a_sc[...] = a
        b_sc[...] = b
        last = 0 if rev else SUBLANES - 1

        def tile_step(t, carry):
            rows = pl.ds(pl.multiple_of(t * SUBLANES, SUBLANES), SUBLANES)
            h = b_sc[rows, :] + a_sc[rows, :] * carry
            if rev:
                h_sc[rows, :] += h
            else:
                h_sc[rows, :] = h
            return jnp.broadcast_to(h[last:last + 1], h.shape)
        carry = jnp.zeros((SUBLANES, xc.shape[1]), F32)
        tc, ta = n_c // SUBLANES, n_all // SUBLANES
        if rev:
            carry = lax.fori_loop(0, tc, lambda t, c: tile_step(tc - 1 - t, c), carry)
            lax.fori_loop(0, ta - tc, lambda t, c: tile_step(ta - 1 - t, c), carry)
        else:
            lax.fori_loop(0, ta, tile_step, carry)
    oc_ref[...] = (h_sc[0:n_c, :] * jax.nn.gelu(gc_ref[...])).astype(oc_ref.dtype)
    ol_ref[...] = (h_sc[n_c:n_all, :] * jax.nn.gelu(gl_ref[...])).astype(ol_ref.dtype)


def _rglru_pallas(proj, p, geo):
    n_lat, seq, bsz, n_ctx = geo
    w = BRANCH_W
    ctx_len = n_ctx // bsz
    cblk = LRU_CBLK
    ncb = w // cblk
    gpb = cblk // LRU_BLOCK_W
    lat_rows, ctx_rows = seq // ctx_len, n_lat // ctx_len
    xcol, gcol = 8 * ncb, 9 * ncb
    vec = lambda t: t.reshape(2, 1, w)
    return pl.pallas_call(
        _lru_kernel,
        grid=(bsz, ncb),
        in_specs=[pl.BlockSpec((seq, cblk), lambda b, c: (b, xcol + c)),
                  pl.BlockSpec((ctx_len, cblk), lambda b, c: (ctx_rows + b, xcol + c)),
                  pl.BlockSpec((seq, cblk), lambda b, c: (b, gcol + c)),
                  pl.BlockSpec((ctx_len, cblk), lambda b, c: (ctx_rows + b, gcol + c)),
                  pl.BlockSpec((4, cblk), lambda b, c: (0, c)),
                  pl.BlockSpec((1, cblk), lambda b, c: (0, c)),
                  pl.BlockSpec((2, gpb, LRU_BLOCK_W, LRU_BLOCK_W), lambda b, c: (0, c, 0, 0)),
                  pl.BlockSpec((2, 1, cblk), lambda b, c: (0, 0, c)),
                  pl.BlockSpec((2, gpb, LRU_BLOCK_W, LRU_BLOCK_W), lambda b, c: (0, c, 0, 0)),
                  pl.BlockSpec((2, 1, cblk), lambda b, c: (0, 0, c)),
                  pl.BlockSpec((2, 1, cblk), lambda b, c: (0, 0, c))],
        out_specs=[pl.BlockSpec((seq, cblk), lambda b, c: (b, c)),
                   pl.BlockSpec((ctx_len, cblk), lambda b, c: (b, c))],
        out_shape=[jax.ShapeDtypeStruct((n_lat, w), BF16), jax.ShapeDtypeStruct((n_ctx, w), BF16)],
        scratch_shapes=[pltpu.VMEM((seq + ctx_len, cblk), F32)] * 3,
        compiler_params=_params(("parallel", "parallel")),
        name="rglru",
    )(proj, proj, proj, proj, p['lru_conv_w'], p['lru_conv_b'].reshape(1, w), p['lru_w_rg'].astype(BF16), vec(p['lru_b_rg']),
      p['lru_w_ig'].astype(BF16), vec(p['lru_b_ig']), vec(p['lru_lam']))


DN_PREP_CBLK = 512
DN_HPB = 2
DN_DOUBLINGS = 5


def _dn_prep_kernel(xl_ref, xc_ref, cw_ref, ol_ref, oc_ref):
    cw = cw_ref[...]
    cblk = cw.shape[1]
    cb = pl.program_id(1)
    n_q = BRANCH_W // cblk

    def prep(x):
        n = x.shape[0]
        row = lax.broadcasted_iota(jnp.int32, x.shape, 0)
        xm2 = jnp.where(row >= 2, pltpu.roll(x, 2, axis=0), 0.0)
        xm1 = jnp.where(row >= 1, pltpu.roll(x, 1, axis=0), 0.0)
        xp1 = jnp.where(row < n - 1, pltpu.roll(x, n - 1, axis=0), 0.0)
        y = cw[0:1] * xm2 + cw[1:2] * xm1 + cw[2:3] * x + cw[3:4] * xp1
        y = y * jax.nn.sigmoid(y)
        heads = [y[:, h * DN_HEAD_DIM:(h + 1) * DN_HEAD_DIM] for h in range(cblk // DN_HEAD_DIM)]
        normed = jnp.concatenate([t * lax.rsqrt(jnp.sum(t * t, axis=-1, keepdims=True) + 1e-6) for t in heads], axis=1)
        scale = jnp.where(cb < n_q, DN_HEAD_DIM ** -0.5, 1.0)
        return jnp.where(cb < 2 * n_q, normed * scale, y)
    ol_ref[...] = prep(xl_ref[...]).astype(ol_ref.dtype)
    oc_ref[...] = prep(xc_ref[...]).astype(oc_ref.dtype)


def _dn_kernel(ql_ref, qc_ref, kl_ref, kc_ref, vl_ref, vc_ref, zl_ref, zc_ref, abl_ref, abc_ref, prm_ref, ol_ref, oc_ref,
               q_sc, k_sc, v_sc, g_sc, b_sc, o_sc):
    n_l, n_c = ql_ref.shape[0], qc_ref.shape[0]
    n_all = n_l + n_c
    c = DN_CHUNK
    hd = DN_HEAD_DIM
    ncc, nch = n_c // c, n_all // c
    hp = pl.program_id(1)
    for sc, rc, rl in ((q_sc, qc_ref, ql_ref), (k_sc, kc_ref, kl_ref), (v_sc, vc_ref, vl_ref)):
        sc[0:n_c, :] = rc[...]
        sc[n_c:n_all, :] = rl[...]
    ab = jnp.concatenate([abc_ref[...], abl_ref[...]], axis=0)
    g_all = prm_ref[0:1, :] * jax.nn.softplus(ab + prm_ref[1:2, :])
    beta_all = jax.nn.sigmoid(ab)
    lane = lax.broadcasted_iota(jnp.int32, ab.shape, 1)
    pos = lax.broadcasted_iota(jnp.int32, ab.shape, 0) % c
    for d in range(2):
        for s in range(DN_HPB):
            col = d * DN_HEADS + hp * DN_HPB + s
            g = jnp.broadcast_to(jnp.sum(jnp.where(lane == col, g_all, 0.0), axis=-1, keepdims=True), ab.shape)
            step = 1
            while step < c:
                if d == 0:
                    g = g + jnp.where(pos >= step, pltpu.roll(g, step, axis=0), 0.0)
                else:
                    g = g + jnp.where(pos < c - step, pltpu.roll(g, n_all - step, axis=0), 0.0)
                step *= 2
            g_sc[d, s] = g
            b_sc[d, s] = jnp.broadcast_to(jnp.sum(jnp.where(lane == 2 * DN_HEADS + col, beta_all, 0.0), axis=-1, keepdims=True), ab.shape)
    ri = lax.broadcasted_iota(jnp.int32, (c, c), 0)
    ci = lax.broadcasted_iota(jnp.int32, (c, c), 1)
    eye = (ri == ci).astype(F32)
    incl = (ri >= ci, ri <= ci)
    strict = (ri > ci, ri < ci)
    nt = (((1,), (1,)), ((), ()))

    def chunk(d, s, n, state):
        rows = pl.ds(pl.multiple_of(n * c, c), c)
        hs = slice(s * hd, (s + 1) * hd)
        q, k, v = q_sc[rows, hs], k_sc[rows, hs], v_sc[rows, hs]
        gc, beta = g_sc[d, s, rows, :], b_sc[d, s, rows, :]
        qf, kf, vf = q.astype(F32), k.astype(F32), v.astype(F32)
        diff = gc[:, :c] - gc.T[:c, :]
        decay = jnp.exp(jnp.where(incl[d], diff, -jnp.inf))
        kb = kf * beta
        a = jnp.where(strict[d], lax.dot_general(kb.astype(BF16), k, nt, preferred_element_type=F32), 0.0) * decay
        x, pw = eye - a, a
        for _ in range(DN_DOUBLINGS):
            pb = pw.astype(BF16)
            pw = jnp.dot(pb, pb, preferred_element_type=F32)
            x = x + jnp.dot(x.astype(BF16), pw.astype(BF16), preferred_element_type=F32)
        eg = jnp.exp(gc)
        uw = jnp.dot(x.astype(BF16), jnp.concatenate([vf * beta, kb * eg], axis=1).astype(BF16), preferred_element_type=F32)
        u, w = uw[:, :hd], uw[:, hd:]
        qk = lax.dot_general(q, k, nt, preferred_element_type=F32) * decay
        sb = state.astype(BF16)
        v_new = u - jnp.dot(w.astype(BF16), sb, preferred_element_type=F32)
        vb = v_new.astype(BF16)
        o = jnp.dot((qf * eg).astype(BF16), sb, preferred_element_type=F32) + jnp.dot(qk.astype(BF16), vb, preferred_element_type=F32)
        g_last = gc[c - 1:c] if d == 0 else gc[0:1]
        k_dec = kf * jnp.exp(g_last - gc)
        o_sc[d, rows, hs] = o
        return state * jnp.exp(g_last) + jnp.dot(k_dec.T.astype(BF16), vb, preferred_element_type=F32)

    def body(t, states):
        n_rev = jnp.where(t < ncc, ncc - 1 - t, nch - 1 - (t - ncc))
        return tuple(chunk(d, s, t if d == 0 else n_rev, states[d * DN_HPB + s]) for d in range(2) for s in range(DN_HPB))
    lax.fori_loop(0, nch, body, tuple(jnp.zeros((hd, hd), F32) for _ in range(2 * DN_HPB)))

    gain = prm_ref[2:3, :]

    def readout(o, z):
        cols = []
        for s in range(DN_HPB):
            t, zz = o[:, s * hd:(s + 1) * hd], z[:, s * hd:(s + 1) * hd]
            cols.append(t * lax.rsqrt(jnp.mean(t * t, axis=-1, keepdims=True) + RMS_EPS) * gain * (zz * jax.nn.sigmoid(zz)))
        return jnp.concatenate(cols, axis=1)
    oc_ref[...] = readout(o_sc[0, 0:n_c, :] + o_sc[1, 0:n_c, :], zc_ref[...]).astype(oc_ref.dtype)
    ol_ref[...] = readout(o_sc[0, n_c:n_all, :] + o_sc[1, n_c:n_all, :], zl_ref[...]).astype(ol_ref.dtype)


def _deltanet_pallas(proj, p, geo):
    n_lat, seq, bsz, n_ctx = geo
    w = BRANCH_W
    ctx_len = n_ctx // bsz
    ctx_rows = n_lat // ctx_len
    cblk = DN_PREP_CBLK
    c0 = w // cblk
    qkv_l, qkv_c = pl.pallas_call(
        _dn_prep_kernel,
        grid=(bsz, 3 * w // cblk),
        in_specs=[pl.BlockSpec((seq, cblk), lambda b, j: (b, c0 + j)),
                  pl.BlockSpec((ctx_len, cblk), lambda b, j: (ctx_rows + b, c0 + j)),
                  pl.BlockSpec((4, cblk), lambda b, j: (0, j))],
        out_specs=[pl.BlockSpec((seq, cblk), lambda b, j: (b, j)),
                   pl.BlockSpec((ctx_len, cblk), lambda b, j: (b, j))],
        out_shape=[jax.ShapeDtypeStruct((n_lat, 3 * w), BF16), jax.ShapeDtypeStruct((n_ctx, 3 * w), BF16)],
        compiler_params=_params(("parallel", "parallel")),
        name="dn_prep",
    )(proj, proj, p['dn_conv_w'])
    hw = DN_HPB * DN_HEAD_DIM
    nhb = w // hw
    zcol, abcol = 4 * w // hw, 10 * w // LANES
    n16 = 2 * DN_HEADS
    prm = (jnp.zeros((8, LANES), F32).at[0, :n16].set(-jnp.exp(p['dn_a_log']).reshape(n16))
           .at[1, :n16].set(p['dn_dt_bias'].reshape(n16)).at[2, :DN_HEAD_DIM].set(p['dn_norm']))
    lat = lambda off: pl.BlockSpec((seq, hw), lambda b, h: (b, off + h))
    ctxs = lambda off: pl.BlockSpec((ctx_len, hw), lambda b, h: (b, off + h))
    n_all = seq + ctx_len
    return pl.pallas_call(
        _dn_kernel,
        grid=(bsz, nhb),
        in_specs=[lat(0), ctxs(0), lat(nhb), ctxs(nhb), lat(2 * nhb), ctxs(2 * nhb),
                  pl.BlockSpec((seq, hw), lambda b, h: (b, zcol + h)),
                  pl.BlockSpec((ctx_len, hw), lambda b, h: (ctx_rows + b, zcol + h)),
                  pl.BlockSpec((seq, LANES), lambda b, h: (b, abcol)),
                  pl.BlockSpec((ctx_len, LANES), lambda b, h: (ctx_rows + b, abcol)),
                  pl.BlockSpec((8, LANES), lambda b, h: (0, 0))],
        out_specs=[pl.BlockSpec((seq, hw), lambda b, h: (b, h)),
                   pl.BlockSpec((ctx_len, hw), lambda b, h: (b, h))],
        out_shape=[jax.ShapeDtypeStruct((n_lat, w), BF16), jax.ShapeDtypeStruct((n_ctx, w), BF16)],
        scratch_shapes=[pltpu.VMEM((n_all, hw), BF16)] * 3
        + [pltpu.VMEM((2, DN_HPB, n_all, LANES), F32)] * 2
        + [pltpu.VMEM((2, n_all, hw), F32)],
        compiler_params=_params(("parallel", "parallel")),
        name="deltanet",
    )(qkv_l, qkv_c, qkv_l, qkv_c, qkv_l, qkv_c, proj, proj, proj, proj, prm)


def _mixers(proj, p, lam_init, geo, rows):
    n_lat = geo[0]
    with_ctx = rows > n_lat
    dn_l, dn_c = _deltanet_pallas(proj, p, geo)
    lru_l, lru_c = _rglru_pallas(proj, p, geo)

    def join(lat, ctx_part):
        return jnp.concatenate([lat, ctx_part], axis=0) if with_ctx else lat
    return [_s5_mixer_pallas(proj, p, geo, rows), join(dn_l, dn_c), _diff_attn_pallas(proj, p, lam_init, geo, rows), join(lru_l, lru_c)]


def kernel(x, c, ctx, c_ctx, ada_w, ada_b, mix_norm, ffn_norm, w_in, s5_lam_re, s5_lam_im, s5_log_dt, s5_b_re, s5_b_im, s5_c_re, s5_c_im, s5_d, s5_w_glu, dn_conv_w, dn_a_log, dn_dt_bias, dn_norm, da_lam_q1, da_lam_k1, da_lam_q2, da_lam_k2, da_subln, lru_conv_w, lru_conv_b, lru_w_rg, lru_b_rg, lru_w_ig, lru_b_ig, lru_lam, w_branch, w_out, router_w, router_b, moe_w_gu, moe_b_gu, moe_w_down, moe_b_down, final_norm):
    bsz, seq, d = x.shape
    n_ctx_tok = ctx.shape[1]
    depth = ada_w.shape[0]
    n_ctx = bsz * n_ctx_tok
    n_lat = bsz * seq
    n_tok = n_ctx + n_lat
    geo = (n_lat, seq, bsz, n_ctx)
    w = BRANCH_W
    mix_cols = 10 * w + 4 * DN_HEADS
    ab_lo = 5 * w

    xs = jnp.concatenate([x.reshape(n_lat, d), ctx.reshape(n_ctx, d)], axis=0)
    mods = _ada_table(c, c_ctx, ada_w, ada_b)

    for l in range(depth):
        with_ctx = l < depth - 1
        lam_init = 0.8 - 0.6 * math.exp(-0.3 * l)
        rows = n_tok if with_ctx else n_lat
        mod = mods[l]
        p = {'s5_lam_re': s5_lam_re[l], 's5_lam_im': s5_lam_im[l], 's5_log_dt': s5_log_dt[l], 's5_b_re': s5_b_re[l], 's5_b_im': s5_b_im[l], 's5_c_re': s5_c_re[l], 's5_c_im': s5_c_im[l], 's5_d': s5_d[l], 's5_w_glu': s5_w_glu[l], 'dn_conv_w': dn_conv_w[l], 'dn_a_log': dn_a_log[l], 'dn_dt_bias': dn_dt_bias[l], 'dn_norm': dn_norm[l], 'da_lam_q1': da_lam_q1[l], 'da_lam_k1': da_lam_k1[l], 'da_lam_q2': da_lam_q2[l], 'da_lam_k2': da_lam_k2[l], 'da_subln': da_subln[l], 'lru_conv_w': lru_conv_w[l], 'lru_conv_b': lru_conv_b[l], 'lru_w_rg': lru_w_rg[l], 'lru_b_rg': lru_b_rg[l], 'lru_w_ig': lru_w_ig[l], 'lru_b_ig': lru_b_ig[l], 'lru_lam': lru_lam[l]}
        wl = w_in[l]
        w_mix = jnp.concatenate([wl[:, :ab_lo], wl[:, ab_lo + 4 * DN_HEADS:mix_cols], wl[:, ab_lo:ab_lo + 4 * DN_HEADS],
                                 jnp.zeros((d, AB_COLS - 4 * DN_HEADS), F32)], axis=1).astype(BF16)
        w_gate = wl[:, mix_cols:].astype(BF16)
        wg = moe_w_gu[l][:, :, 0::2].astype(BF16)
        wu = moe_w_gu[l][:, :, 1::2].astype(BF16)
        bg = moe_b_gu[l][:, None, 0::2]
        bu = moe_b_gu[l][:, None, 1::2]
        wd = moe_w_down[l].astype(BF16)
        bd = moe_b_down[l][:, None, :]

        h = _norm_mod(xs, mix_norm[l], mod, geo)
        proj = _matmul(h, w_mix, F32)
        branches = _mixers(proj, p, lam_init, geo, rows)
        merged = _merge(h, branches, w_gate, w_branch[l].astype(BF16), rows)
        xs = _outproj_residual(merged, w_out[l].astype(BF16), xs, mod, geo, rows)
        final = l == depth - 1
        xs = _moe_layer(xs, ffn_norm[l], mod, router_w[l], router_b[l], wg, wu, bg, bu, wd, bd, final_norm, geo, rows, final)
    return xs.reshape(bsz, seq, d)
```

```python
import functools
import math

import jax
import jax.numpy as jnp
import numpy as np
from jax import lax
from jax.experimental import pallas as pl
from jax.experimental.pallas import tpu as pltpu

F32 = jnp.float32
BF16 = jnp.bfloat16

GRID_W = 64
N_BRANCH = 4
BRANCH_W = 1024
RMS_EPS = 1e-6
N_MOD = 6
ROPE_BASE = 10000.0
S5_GROUP = 16
S5_GROUPS = BRANCH_W // S5_GROUP
S5_STATE = 64
DN_HEADS = 8
DN_HEAD_DIM = BRANCH_W // DN_HEADS
DN_CHUNK = 64
DA_HEADS = 8
DA_HEAD_DIM = BRANCH_W // (2 * DA_HEADS)
DA_V_DIM = 2 * DA_HEAD_DIM
Q_BLOCK = 128
LRU_BLOCKS = 8
LRU_BLOCK_W = BRANCH_W // LRU_BLOCKS
LRU_C = 8.0
TOP_K = 4
SWIGLU_LIMIT = 7.0
SWIGLU_ALPHA = 1.702

MOD_ROWS = 8
LANES = 128
MOE_TILE = 256
AB_COLS = 128
VMEM_LIMIT = 56 << 20


def _params(sem, vmem=VMEM_LIMIT):
    return pltpu.CompilerParams(dimension_semantics=sem, vmem_limit_bytes=vmem)


def _tile(n, pref):
    t = min(n, pref)
    while n % t:
        t //= 2
    return t


def _ada_kernel(c_ref, w_ref, b_ref, o_ref):
    @pl.when(pl.program_id(2) == 0)
    def _():
        o_ref[0] = jnp.broadcast_to(b_ref[0], o_ref.shape[1:])
    c = c_ref[...]
    s = (c * jax.nn.sigmoid(c)).astype(BF16)
    o_ref[0] += jnp.dot(s, w_ref[0].astype(BF16), preferred_element_type=F32)


def _ada_table(c, c_ctx, ada_w, ada_b):
    depth, d, nm = ada_w.shape
    bsz = c.shape[0]
    cs = jnp.zeros((MOD_ROWS, d), F32).at[:bsz].set(c).at[bsz].set(c_ctx)
    tk, tn = _tile(d, 1024), _tile(nm, 2048)
    mod = pl.pallas_call(
        _ada_kernel,
        grid=(depth, nm // tn, d // tk),
        in_specs=[pl.BlockSpec((MOD_ROWS, tk), lambda l, n, k: (0, k)),
                  pl.BlockSpec((1, tk, tn), lambda l, n, k: (l, k, n)),
                  pl.BlockSpec((1, 1, tn), lambda l, n, k: (l, 0, n))],
        out_specs=pl.BlockSpec((1, MOD_ROWS, tn), lambda l, n, k: (l, 0, n)),
        out_shape=jax.ShapeDtypeStruct((depth, MOD_ROWS, nm), F32),
        compiler_params=_params(("parallel", "parallel", "arbitrary")),
        name="ada_table",
    )(cs, ada_w, ada_b.reshape(depth, 1, nm))
    return jnp.moveaxis(mod.reshape(depth, MOD_ROWS, N_MOD, d), 1, 2).reshape(depth, N_MOD * MOD_ROWS, 1, d)


def _mod_index(which, tm, geo):
    n_lat, seq, bsz, _ = geo

    def idx(i):
        r = i * tm
        return which * MOD_ROWS + jnp.where(r < n_lat, r // seq, bsz)
    return idx


def _row_tile(geo, pref):
    return _tile(math.gcd(geo[1], geo[3]), pref)


def _norm_mod_kernel(x_ref, g_ref, sh_ref, sc_ref, o_ref):
    x = x_ref[...]
    y = x * lax.rsqrt(jnp.mean(x * x, axis=-1, keepdims=True) + RMS_EPS) * g_ref[...]
    o_ref[...] = (y * (1.0 + sc_ref[0]) + sh_ref[0]).astype(o_ref.dtype)


def _norm_mod(x, gain, mod, geo):
    n, d = x.shape
    tm = _row_tile(geo,256)
    sh, sc = _mod_index(0, tm, geo), _mod_index(1, tm, geo)
    return pl.pallas_call(
        _norm_mod_kernel,
        grid=(n // tm,),
        in_specs=[pl.BlockSpec((tm, d), lambda i: (i, 0)),
                  pl.BlockSpec((1, d), lambda i: (0, 0)),
                  pl.BlockSpec((1, 1, d), lambda i: (sh(i), 0, 0)),
                  pl.BlockSpec((1, 1, d), lambda i: (sc(i), 0, 0))],
        out_specs=pl.BlockSpec((tm, d), lambda i: (i, 0)),
        out_shape=jax.ShapeDtypeStruct((n, d), BF16),
        compiler_params=_params(("parallel",)),
        name="norm_mod",
    )(x, gain.reshape(1, d), mod, mod)


def _mm_kernel(a_ref, b_ref, o_ref):
    o_ref[...] = jnp.dot(a_ref[...], b_ref[...], preferred_element_type=F32).astype(o_ref.dtype)


def _matmul(a, b, out_dtype, tm_pref=512, tn_pref=1152):
    m, k = a.shape
    n = b.shape[1]
    tm = _tile(m, tm_pref)
    tn = tn_pref if n % tn_pref == 0 else _tile(n, 1024)
    return pl.pallas_call(
        _mm_kernel,
        grid=(m // tm, n // tn),
        in_specs=[pl.BlockSpec((tm, k), lambda i, j: (i, 0)),
                  pl.BlockSpec((k, tn), lambda i, j: (0, j))],
        out_specs=pl.BlockSpec((tm, tn), lambda i, j: (i, j)),
        out_shape=jax.ShapeDtypeStruct((m, n), out_dtype),
        compiler_params=_params(("parallel", "parallel")),
        name="in_proj",
    )(a, b)


def _merge_kernel(h_ref, b0, b1, b2, b3, g0, g1, g2, g3, wb_ref, o_ref):
    h = h_ref[...]
    acc = None
    for k, (br, wg) in enumerate(((b0, g0), (b1, g1), (b2, g2), (b3, g3))):
        gate = jax.nn.sigmoid(jnp.dot(h, wg[...], preferred_element_type=F32))
        lifted = jnp.dot(br[...], wb_ref[k], preferred_element_type=F32)
        acc = gate * lifted if acc is None else acc + gate * lifted
    o_ref[...] = acc.astype(o_ref.dtype)


def _merge(h, branches, w_gate, w_branch, rows):
    d = h.shape[1]
    w = branches[0].shape[1]
    tm, tn = _tile(rows, 512), _tile(d, 256)
    nt = d // tn
    gate_specs = [pl.BlockSpec((d, tn), functools.partial(lambda i, j, k: (0, k * nt + j), k=k)) for k in range(N_BRANCH)]
    return pl.pallas_call(
        _merge_kernel,
        grid=(rows // tm, nt),
        in_specs=[pl.BlockSpec((tm, d), lambda i, j: (i,0))]
        + [pl.BlockSpec((tm, w), lambda i, j: (i,0))] * N_BRANCH
        + gate_specs
        + [pl.BlockSpec((N_BRANCH, w, tn), lambda i, j: (0, 0, j))],
        out_specs=pl.BlockSpec((tm, tn), lambda i, j: (i, j)),
        out_shape=jax.ShapeDtypeStruct((rows, d), BF16),
        compiler_params=_params(("parallel", "parallel")),
        name="merge",
    )(h, *branches, w_gate, w_gate, w_gate, w_gate, w_branch)


def _outproj_kernel(m_ref, w_ref, x_ref, g_ref, o_ref):
    o_ref[...] = x_ref[...] + g_ref[0] * jnp.dot(m_ref[...], w_ref[...], preferred_element_type=F32)


def _outproj_residual(merged, w_out, x, mod, geo, rows):
    n, d = x.shape
    tm, tn = _row_tile(geo,512), _tile(d, 1024)
    gi = _mod_index(2, tm, geo)
    return pl.pallas_call(
        _outproj_kernel,
        grid=(rows // tm, d // tn),
        in_specs=[pl.BlockSpec((tm, d), lambda i, j: (i, 0)),
                  pl.BlockSpec((d, tn), lambda i, j: (0, j)),
                  pl.BlockSpec((tm, tn), lambda i, j: (i,j)),
                  pl.BlockSpec((1, 1, tn), lambda i, j: (gi(i), 0, j))],
        out_specs=pl.BlockSpec((tm, tn), lambda i, j: (i, j)),
        out_shape=jax.ShapeDtypeStruct((rows, d), F32),
        compiler_params=_params(("parallel", "parallel")),
        name="out_proj",
    )(merged, w_out, x, mod)


def _router_kernel(n_experts, x_ref, g_ref, sh_ref, sc_ref, rw_ref, rb_ref, h_ref, ti_ref, tw_ref):
    x = x_ref[...]
    y = x * lax.rsqrt(jnp.mean(x * x, axis=-1, keepdims=True) + RMS_EPS) * g_ref[...]
    h = y * (1.0 + sc_ref[0]) + sh_ref[0]
    h_ref[...] = h
    logits = jnp.dot(h.astype(BF16), rw_ref[...], preferred_element_type=F32) + rb_ref[...]
    lane = lax.broadcasted_iota(jnp.int32, logits.shape, 1)
    logits = jnp.where(lane < n_experts, logits, -jnp.inf)
    idx_out = jnp.zeros(logits.shape, jnp.int32)
    val_out = jnp.zeros(logits.shape, F32)
    top = None
    denom = None
    for k in range(TOP_K):
        m = jnp.max(logits, axis=-1, keepdims=True)
        idx = jnp.min(jnp.where(logits == m, lane, LANES), axis=-1, keepdims=True)
        top = m if top is None else top
        e = jnp.exp(m - top)
        denom = e if denom is None else denom + e
        idx_out = jnp.where(lane == k, idx, idx_out)
        val_out = jnp.where(lane == k, e, val_out)
        logits = jnp.where(lane == idx, -jnp.inf, logits)
    ti_ref[...] = idx_out
    tw_ref[...] = val_out / denom


def _router(x, gain, mod, router_w, router_b, geo, rows):
    n, d = x.shape
    n_experts = router_w.shape[1]
    tm = _row_tile(geo,256)
    sh, sc = _mod_index(3, tm, geo), _mod_index(4, tm, geo)
    rw = jnp.zeros((d, LANES), BF16).at[:, :n_experts].set(router_w.astype(BF16))
    rb = jnp.zeros((1, LANES), F32).at[0, :n_experts].set(router_b)
    return pl.pallas_call(
        functools.partial(_router_kernel, n_experts),
        grid=(rows // tm,),
        in_specs=[pl.BlockSpec((tm, d), lambda i: (i, 0)),
                  pl.BlockSpec((1, d), lambda i: (0, 0)),
                  pl.BlockSpec((1, 1, d), lambda i: (sh(i), 0, 0)),
                  pl.BlockSpec((1, 1, d), lambda i: (sc(i), 0, 0)),
                  pl.BlockSpec((d, LANES), lambda i: (0, 0)),
                  pl.BlockSpec((1, LANES), lambda i: (0, 0))],
        out_specs=[pl.BlockSpec((tm, d), lambda i: (i, 0)),
                   pl.BlockSpec((tm, LANES), lambda i: (i, 0)),
                   pl.BlockSpec((tm, LANES), lambda i: (i, 0))],
        out_shape=[jax.ShapeDtypeStruct((rows, d), F32),
                   jax.ShapeDtypeStruct((rows, LANES), jnp.int32),
                   jax.ShapeDtypeStruct((rows, LANES), F32)],
        compiler_params=_params(("parallel",)),
        name="router",
    )(x, gain.reshape(1, d), mod, mod, rw, rb)


def _row_copy(src_hbm, row, dst, r, sem):
    return pltpu.make_async_copy(src_hbm.at[pl.ds(row, 1)], dst.at[pl.ds(r, 1)], sem)


def _expert_kernel(te_ref, used_ref, src_ref, h_hbm, wgu_ref, bgu_ref, wd_ref, bd_ref, o_ref, xbuf, sem):
    g = pl.program_id(0)
    n_used = used_ref[0]
    slot = g % 2

    def issue(tile, s):
        def body(r, carry):
            _row_copy(h_hbm, src_ref[tile * MOE_TILE + r], xbuf.at[s], r, sem.at[s]).start()
            return carry
        lax.fori_loop(0, MOE_TILE, body, 0, unroll=8)

    def wait(s):
        def body(r, carry):
            _row_copy(h_hbm, 0, xbuf.at[s], r, sem.at[s]).wait()
            return carry
        lax.fori_loop(0, MOE_TILE, body, 0, unroll=8)

    @pl.when((g == 0) & (n_used > 0))
    def _():
        issue(0, 0)

    @pl.when(g + 1 < n_used)
    def _():
        issue(g + 1, 1 - slot)

    @pl.when(g < n_used)
    def _():
        wait(slot)
        x = xbuf[slot].astype(BF16)
        gu = jnp.dot(x, wgu_ref[0], preferred_element_type=F32) + bgu_ref[0]
        gate = jnp.minimum(gu, SWIGLU_LIMIT)
        up = jnp.clip(pltpu.roll(gu, gu.shape[1] - 1, axis=1), -SWIGLU_LIMIT, SWIGLU_LIMIT)
        act = (up + 1.0) * (gate * jax.nn.sigmoid(SWIGLU_ALPHA * gate))
        even = lax.broadcasted_iota(jnp.int32, act.shape, 1) % 2 == 0
        act = jnp.where(even, act, 0.0)
        o_ref[...] = jnp.dot(act.astype(BF16), wd_ref[0], preferred_element_type=F32) + bd_ref[0]

    @pl.when(g >= n_used)
    def _():
        o_ref[...] = jnp.zeros_like(o_ref)


def _expert_ffn(h, tile_expert, n_used, row_src, wgu, bgu, wd, bd):
    d = h.shape[1]
    ff2 = wgu.shape[2]
    n_tiles = tile_expert.shape[0]
    grid_spec = pltpu.PrefetchScalarGridSpec(
        num_scalar_prefetch=3,
        grid=(n_tiles,),
        in_specs=[pl.BlockSpec(memory_space=pl.ANY),
                  pl.BlockSpec((1, d, ff2), lambda g, te, nu, rs: (te[g], 0, 0)),
                  pl.BlockSpec((1, 1, ff2), lambda g, te, nu, rs: (te[g], 0, 0)),
                  pl.BlockSpec((1, ff2, d), lambda g, te, nu, rs: (te[g], 0, 0)),
                  pl.BlockSpec((1, 1, d), lambda g, te, nu, rs: (te[g], 0, 0))],
        out_specs=pl.BlockSpec((MOE_TILE, d), lambda g, te, nu, rs: (g, 0)),
        scratch_shapes=[pltpu.VMEM((2, MOE_TILE, d), F32), pltpu.SemaphoreType.DMA((2,))],
    )
    return pl.pallas_call(
        _expert_kernel,
        grid_spec=grid_spec,
        out_shape=jax.ShapeDtypeStruct((n_tiles * MOE_TILE, d), F32),
        compiler_params=_params(("arbitrary",)),
        name="expert_ffn",
    )(tile_expert, n_used, row_src, h, wgu, bgu, wd, bd)


def _combine_kernel(tm, final, pos_ref, x_ref, tw_ref, g_ref, fn_ref, ys_hbm, o_ref, ybuf, sem):
    i = pl.program_id(0)
    n_steps = pl.num_programs(0)
    slot = i % 2

    def issue(step, s):
        def body(r, carry):
            for k in range(TOP_K):
                _row_copy(ys_hbm, pos_ref[(step * tm + r) * TOP_K + k], ybuf.at[s, k], r, sem.at[s]).start()
            return carry
        lax.fori_loop(0, tm, body, 0, unroll=4)

    def wait(s):
        def body(r, carry):
            for k in range(TOP_K):
                _row_copy(ys_hbm, 0, ybuf.at[s, k], r, sem.at[s]).wait()
            return carry
        lax.fori_loop(0, tm, body, 0, unroll=4)

    @pl.when(i == 0)
    def _():
        issue(0, 0)

    @pl.when(i + 1 < n_steps)
    def _():
        issue(i + 1, 1 - slot)

    wait(slot)
    tw = tw_ref[...]
    acc = tw[:, 0:1] * ybuf[slot, 0]
    for k in range(1, TOP_K):
        acc = acc + tw[:, k:k + 1] * ybuf[slot, k]
    y = x_ref[...] + g_ref[0] * acc
    if final:
        y = y * lax.rsqrt(jnp.mean(y * y, axis=-1, keepdims=True) + RMS_EPS) * fn_ref[...]
    o_ref[...] = y


def _combine(x, pos, tw, ys, mod, final_gain, geo, rows, final):
    n, d = x.shape
    tm = _row_tile(geo,128)
    gi = _mod_index(5, tm, geo)
    grid_spec = pltpu.PrefetchScalarGridSpec(
        num_scalar_prefetch=1,
        grid=(rows // tm,),
        in_specs=[pl.BlockSpec((tm, d), lambda i, p: (i, 0)),
                  pl.BlockSpec((tm, LANES), lambda i, p: (i, 0)),
                  pl.BlockSpec((1, 1, d), lambda i, p: (gi(i), 0, 0)),
                  pl.BlockSpec((1, d), lambda i, p: (0, 0)),
                  pl.BlockSpec(memory_space=pl.ANY)],
        out_specs=pl.BlockSpec((tm, d), lambda i, p: (i, 0)),
        scratch_shapes=[pltpu.VMEM((2, TOP_K, tm, d), F32), pltpu.SemaphoreType.DMA((2,))],
    )
    return pl.pallas_call(
        functools.partial(_combine_kernel, tm, final),
        grid_spec=grid_spec,
        out_shape=jax.ShapeDtypeStruct((rows, d), F32),
        compiler_params=_params(("arbitrary",)),
        name="moe_combine",
    )(pos, x, tw, mod, final_gain.reshape(1, d), ys)


def _moe_layer(x, gain, mod, router_w, router_b, wgu, bgu, wd, bd, final_gain, geo, rows, final):
    n_exp = wgu.shape[0]
    h, top_i, top_w = _router(x, gain, mod, router_w, router_b, geo, rows)
    e_flat = top_i[:, :TOP_K].reshape(-1)
    onehot = (e_flat[:, None] == jnp.arange(n_exp, dtype=jnp.int32)[None, :]).astype(jnp.int32)
    rank = jnp.sum((jnp.cumsum(onehot, axis=0) - onehot) * onehot, axis=1)
    counts = jnp.sum(onehot, axis=0)
    padded = ((counts + MOE_TILE - 1) // MOE_TILE) * MOE_TILE
    ends = jnp.cumsum(padded)
    pos = ((ends - padded)[e_flat] + rank).astype(jnp.int32)
    n_tiles = (rows * TOP_K + n_exp * (MOE_TILE - 1)) // MOE_TILE + 1
    tok = jnp.arange(rows * TOP_K, dtype=jnp.int32) // TOP_K
    row_src = jnp.zeros((n_tiles * MOE_TILE,), jnp.int32).at[pos].set(tok, unique_indices=True)
    tile_start = jnp.arange(n_tiles, dtype=jnp.int32) * MOE_TILE
    tile_expert = jnp.minimum(jnp.searchsorted(ends, tile_start, side="right"), n_exp - 1).astype(jnp.int32)
    n_used = (ends[-1] // MOE_TILE).astype(jnp.int32).reshape(1)
    ys = _expert_ffn(h, tile_expert, n_used, row_src, wgu, bgu, wd, bd)
    return _combine(x, pos, top_w, ys, mod, final_gain, geo, rows, final)


def _flip(t, rev):
    return jnp.flip(t, axis=1) if rev else t


def _linear_combine(e1, e2):
    a1, b1 = e1
    a2, b2 = e2
    return a1 * a2, a2 * b1 + b2


def _dwconv(x, w, b=None):
    k = w.shape[0]
    y = lax.conv_general_dilated(x, w[:, None, :].astype(x.dtype), (1,), [(k // 2, k - 1 - k // 2)], dimension_numbers=('NWC', 'WIO', 'NWC'), feature_group_count=x.shape[-1])
    return y if b is None else y + b.astype(x.dtype)


def _l2norm(t):
    return t * lax.rsqrt(jnp.sum(t * t, axis=-1, keepdims=True) + 1e-6)


def _rmsnorm(x, g):
    xf = x.astype(F32)
    y = xf * lax.rsqrt(jnp.mean(xf * xf, axis=-1, keepdims=True) + RMS_EPS)
    return (y * g.astype(F32)).astype(x.dtype)


def _axial_rope(n_tok, rot_dim):
    n_rows = n_tok // GRID_W
    rows = jnp.repeat(jnp.arange(n_rows), GRID_W).astype(F32)
    cols = jnp.tile(jnp.arange(GRID_W), n_rows).astype(F32)
    n_freq = rot_dim // 4
    inv = ROPE_BASE ** (-jnp.arange(n_freq, dtype=F32) / n_freq)
    ang = jnp.concatenate([rows[:, None] * inv, cols[:, None] * inv], axis=-1)
    return jnp.cos(ang), jnp.sin(ang)


def _rope(x, cos, sin):
    half = x.shape[-1] // 2
    shp = (1, cos.shape[0]) + (1,) * (x.ndim - 3) + (half,)
    c, s = cos.reshape(shp), sin.reshape(shp)
    x1, x2 = x[..., :half], x[..., half:]
    return jnp.concatenate([x1 * c - x2 * s, x1 * s + x2 * c], axis=-1).astype(x.dtype)


def _s5_scan(u, lam_bar, b_bar, c_mat, h0):
    bu = jnp.einsum('blgc,gpc->blgp', u.astype(jnp.complex64), b_bar)
    bu = bu.at[:, 0].add(lam_bar * h0)
    a = jnp.broadcast_to(lam_bar, bu.shape)
    _, h = lax.associative_scan(_linear_combine, (a, bu), axis=1)
    y = jnp.einsum('blgp,gcp->blgc', h, c_mat).real
    return y, h[:, -1]


def _s5_mixer(u_ctx, u_lat, p):
    def groups(u):
        return u.astype(F32).reshape(u.shape[0], u.shape[1], S5_GROUPS, S5_GROUP)
    uc, ul = groups(u_ctx), groups(u_lat)
    y_ctx = jnp.zeros_like(uc)
    y_lat = jnp.zeros_like(ul)
    for d, rev in enumerate((False, True)):
        lam = lax.complex(p['s5_lam_re'][d], p['s5_lam_im'][d])
        dt = jnp.exp(p['s5_log_dt'][d])[:, None]
        lam_bar = jnp.exp(lam * dt)
        b_mat = lax.complex(p['s5_b_re'][d], p['s5_b_im'][d])
        b_bar = ((lam_bar - 1.0) / lam)[..., None] * b_mat
        c_mat = lax.complex(p['s5_c_re'][d], p['s5_c_im'][d])
        h0 = jnp.zeros((uc.shape[0], S5_GROUPS, S5_STATE), jnp.complex64)
        yc, h_ctx = _s5_scan(_flip(uc, rev), lam_bar, b_bar, c_mat, h0)
        yl, _ = _s5_scan(_flip(ul, rev), lam_bar, b_bar, c_mat, h_ctx)
        y_ctx = y_ctx + _flip(yc, rev)
        y_lat = y_lat + _flip(yl, rev)
    d_skip = p['s5_d'].reshape(S5_GROUPS, S5_GROUP)
    w_glu = p['s5_w_glu']

    def readout(y, u):
        z = jax.nn.gelu((y + d_skip * u).reshape(u.shape[0], u.shape[1], BRANCH_W))
        return z * jax.nn.sigmoid(z @ w_glu)
    return readout(y_ctx, uc), readout(y_lat, ul)


def _gated_delta_rule(q, k, v, g, beta, s0):
    bsz, n_tok, n_h, _ = q.shape
    dv = v.shape[-1]
    n_chunk = n_tok // DN_CHUNK

    def chunks(t):
        return jnp.moveaxis(t.reshape(bsz, n_chunk, DN_CHUNK, n_h, -1), (1, 3), (0, 2))
    qc, kc, vc = chunks(q), chunks(k), chunks(v)
    gcum = jnp.cumsum(chunks(g[..., None])[..., 0], axis=-1)
    bc = chunks(beta[..., None])[..., 0]
    causal = jnp.tril(jnp.ones((DN_CHUNK, DN_CHUNK), bool))
    strict = jnp.tril(jnp.ones((DN_CHUNK, DN_CHUNK), bool), -1)
    decay = jnp.exp(jnp.where(causal, gcum[..., :, None] - gcum[..., None, :], -jnp.inf))
    kb = kc * bc[..., None]
    a_low = jnp.where(strict, jnp.einsum('nbhid,nbhjd->nbhij', kb, kc) * decay, 0.0)
    eye = jnp.eye(DN_CHUNK, dtype=F32)
    t_inv = lax.linalg.triangular_solve(a_low + eye, jnp.broadcast_to(eye, a_low.shape), left_side=True, lower=True, unit_diagonal=True)
    u = t_inv @ (vc * bc[..., None])
    w = t_inv @ (kb * jnp.exp(gcum)[..., None])
    qk = jnp.einsum('nbhid,nbhjd->nbhij', qc, kc) * decay

    def step(s, inp):
        q_i, k_i, u_i, w_i, g_i, qk_i = inp
        v_new = u_i - w_i @ s
        o = (q_i * jnp.exp(g_i)[..., None]) @ s + qk_i @ v_new
        g_last = g_i[..., -1:]
        k_dec = k_i * jnp.exp(g_last - g_i)[..., None]
        s = s * jnp.exp(g_last)[..., None] + jnp.einsum('bhcd,bhce->bhde', k_dec, v_new)
        return s, o
    s_fin, o = lax.scan(step, s0, (qc, kc, u, w, gcum, qk))
    o = jnp.moveaxis(o, (0, 2), (1, 3)).reshape(bsz, n_tok, n_h, dv)
    return o, s_fin


def _dn_prep(qkv, a, b, p):
    bsz, n = qkv.shape[:2]
    qkv = jax.nn.silu(_dwconv(qkv, p['dn_conv_w'])).reshape(bsz, n, 3, DN_HEADS, DN_HEAD_DIM)
    q = _l2norm(qkv[:, :, 0]) * DN_HEAD_DIM ** -0.5
    k = _l2norm(qkv[:, :, 1])
    v = qkv[:, :, 2]
    a = a.reshape(bsz, n, 2, DN_HEADS)
    g = -jnp.exp(p['dn_a_log']) * jax.nn.softplus(a + p['dn_dt_bias'])
    beta = jax.nn.sigmoid(b.reshape(bsz, n, 2, DN_HEADS))
    return q, k, v, g, beta


def _deltanet_mixer(qkv_c, a_c, b_c, z_c, qkv_l, a_l, b_l, z_l, p):
    ctx_in = _dn_prep(qkv_c, a_c, b_c, p)
    lat_in = _dn_prep(qkv_l, a_l, b_l, p)
    bsz = qkv_l.shape[0]
    o_ctx = jnp.zeros_like(ctx_in[2])
    o_lat = jnp.zeros_like(lat_in[2])
    for d, rev in enumerate((False, True)):
        def dir_inputs(t):
            q, k, v, g, beta = t
            return [_flip(z, rev) for z in (q, k, v, g[:, :, d], beta[:, :, d])]
        s0 = jnp.zeros((bsz, DN_HEADS, DN_HEAD_DIM, DN_HEAD_DIM), F32)
        oc, s_ctx = _gated_delta_rule(*dir_inputs(ctx_in), s0)
        ol, _ = _gated_delta_rule(*dir_inputs(lat_in), s_ctx)
        o_ctx = o_ctx + _flip(oc, rev)
        o_lat = o_lat + _flip(ol, rev)

    def readout(o, z):
        y = _rmsnorm(o, p['dn_norm']) * jax.nn.silu(z.reshape(o.shape))
        return y.reshape(o.shape[0], o.shape[1], BRANCH_W)
    return readout(o_ctx, z_c), readout(o_lat, z_l)


def _diff_attn_mixer(q_c, k_c, v_c, q_l, k_l, v_l, p, lam_init):
    bsz, n_lat = q_l.shape[:2]

    def qk_heads(t):
        return t.reshape(t.shape[0], t.shape[1], DA_HEADS, 2, DA_HEAD_DIM)

    def v_heads(t):
        return t.reshape(t.shape[0], t.shape[1], DA_HEADS, DA_V_DIM)
    cos, sin = _axial_rope(n_lat, DA_HEAD_DIM)
    qc, kc, vc = qk_heads(q_c), qk_heads(k_c), v_heads(v_c)
    ql, kl, vl = _rope(qk_heads(q_l), cos, sin), _rope(qk_heads(k_l), cos, sin), v_heads(v_l)
    lam = (jnp.exp(jnp.sum(p['da_lam_q1'] * p['da_lam_k1'])) - jnp.exp(jnp.sum(p['da_lam_q2'] * p['da_lam_k2'])) + lam_init)
    scale = DA_HEAD_DIM ** -0.5

    def attend(q, k, v):
        s = jnp.einsum('bqhmd,bkhmd->bhmqk', q, k) * scale
        pr = jax.nn.softmax(s, axis=-1)
        w = pr[:, :, 0] - lam * pr[:, :, 1]
        return jnp.einsum('bhqk,bkhe->bqhe', w, v)
    o_ctx = attend(qc, kc, vc)
    k_all = jnp.concatenate([kl, kc], axis=1)
    v_all = jnp.concatenate([vl, vc], axis=1)
    n_blk = n_lat // Q_BLOCK
    q_blocks = jnp.moveaxis(ql.reshape(bsz, n_blk, Q_BLOCK, DA_HEADS, 2, DA_HEAD_DIM), 1, 0)
    o_lat = lax.map(lambda qb: attend(qb, k_all, v_all), q_blocks)
    o_lat = jnp.moveaxis(o_lat, 0, 1).reshape(bsz, n_lat, DA_HEADS, DA_V_DIM)

    def readout(o):
        y = _rmsnorm(o, p['da_subln']) * (1.0 - lam_init)
        return y.reshape(o.shape[0], o.shape[1], BRANCH_W)
    return readout(o_ctx), readout(o_lat)


def _rglru_scan(x, w_r, b_r, w_i, b_i, lam, h0):
    bsz, n, _ = x.shape
    xb = x.reshape(bsz, n, LRU_BLOCKS, LRU_BLOCK_W)
    r = jax.nn.sigmoid(jnp.einsum('blhi,hij->blhj', xb, w_r).reshape(bsz, n, BRANCH_W) + b_r)
    i = jax.nn.sigmoid(jnp.einsum('blhi,hij->blhj', xb, w_i).reshape(bsz, n, BRANCH_W) + b_i)
    log_a = -LRU_C * r * jax.nn.softplus(-lam)
    a = jnp.exp(log_a)
    b = jnp.sqrt(-jnp.expm1(2.0 * log_a)) * (i * x)
    b = b.at[:, 0].add(a[:, 0] * h0)
    _, h = lax.associative_scan(_linear_combine, (a, b), axis=1)
    return h, h[:, -1]


def _rglru_mixer(x_c, g_c, x_l, g_l, p):
    xc = _dwconv(x_c, p['lru_conv_w'], p['lru_conv_b'])
    xl = _dwconv(x_l, p['lru_conv_w'], p['lru_conv_b'])
    h_ctx = jnp.zeros_like(xc)
    h_lat = jnp.zeros_like(xl)
    for d, rev in enumerate((False, True)):
        prm = tuple(p[name][d] for name in ('lru_w_rg', 'lru_b_rg', 'lru_w_ig', 'lru_b_ig', 'lru_lam'))
        h0 = jnp.zeros((xc.shape[0], BRANCH_W), F32)
        hc, h_fin = _rglru_scan(_flip(xc, rev), *prm, h0)
        hl, _ = _rglru_scan(_flip(xl, rev), *prm, h_fin)
        h_ctx = h_ctx + _flip(hc, rev)
        h_lat = h_lat + _flip(hl, rev)
    return h_ctx * jax.nn.gelu(g_c), h_lat * jax.nn.gelu(g_l)


S5_CHUNK = 16
S5_ROW = S5_CHUNK * S5_GROUP
S5_GBLK = 8


def _s5_operators(p, n_chunks):
    hi = lax.Precision.HIGHEST
    n_steps = max(1, (n_chunks - 1).bit_length())
    ci = jnp.arange(S5_CHUNK)
    t_ops, in_ops, out_ops, scan_ops = [], [], [], []
    for d in range(2):
        lam = lax.complex(p['s5_lam_re'][d], p['s5_lam_im'][d])
        dt = jnp.exp(p['s5_log_dt'][d])[:, None]
        lam_bar = jnp.exp(lam * dt)
        b_bar = ((lam_bar - 1.0) / lam)[..., None] * lax.complex(p['s5_b_re'][d], p['s5_b_im'][d])
        c_mat = lax.complex(p['s5_c_re'][d], p['s5_c_im'][d])
        pw = [jnp.ones_like(lam_bar)]
        for _ in range(S5_CHUNK):
            pw.append(pw[-1] * lam_bar)
        pw = jnp.stack(pw)
        kern = jnp.einsum('gop,mgp,gpi->mgoi', c_mat, pw[:S5_CHUNK], b_bar, precision=hi).real
        lag = (ci[None, :] - ci[:, None]) if d == 0 else (ci[:, None] - ci[None, :])
        t_op = jnp.where((lag >= 0)[:, :, None, None, None], kern[jnp.clip(lag, 0, S5_CHUNK - 1)], 0.0)
        t_ops.append(jnp.transpose(t_op, (2, 0, 4, 1, 3)).reshape(S5_GROUPS, S5_ROW, S5_ROW))
        e_in = (S5_CHUNK - 1 - ci) if d == 0 else ci
        m_in = pw[e_in][:, :, :, None] * b_bar[None]
        m_in = jnp.transpose(m_in, (1, 0, 3, 2)).reshape(S5_GROUPS, S5_ROW, S5_STATE)
        in_ops.append(jnp.concatenate([m_in.real, m_in.imag], axis=-1))
        e_out = (ci + 1) if d == 0 else (S5_CHUNK - ci)
        m_out = c_mat[None] * pw[e_out][:, :, None, :]
        m_out = jnp.transpose(m_out, (1, 3, 0, 2)).reshape(S5_GROUPS, S5_STATE, S5_ROW)
        out_ops.append(jnp.concatenate([m_out.real, -m_out.imag], axis=1))
        a = pw[S5_CHUNK]
        mults = []
        for _ in range(n_steps):
            mults.append(a)
            a = a * a
        mults = jnp.stack(mults, axis=1)
        scan_ops.append(jnp.concatenate([jnp.concatenate([mults.real, mults.real], -1), jnp.concatenate([-mults.imag, mults.imag], -1)], axis=1))
    return (jnp.stack(t_ops).astype(BF16), jnp.stack(in_ops).astype(BF16), jnp.stack(out_ops).astype(BF16), jnp.stack(scan_ops))


def _s5_kernel(nb, n_steps, ul_ref, uc_ref, t_ref, in_ref, out_ref, sc_ref, yl_ref, yc_ref):
    n_l, n_c = ul_ref.shape[1], uc_ref.shape[1]

    def shift(x, rows, reverse):
        n = x.shape[0]
        row = lax.broadcasted_iota(jnp.int32, x.shape, 0)
        if reverse:
            return jnp.where(row < n - rows, pltpu.roll(x, n - rows, axis=0), 0.0)
        return jnp.where(row >= rows, pltpu.roll(x, rows, axis=0), 0.0)

    def group(g, carry):
        ul, uc = ul_ref[g], uc_ref[g]
        ys = []
        for d in range(2):
            rev = d == 1
            u = jnp.concatenate([ul, uc], axis=0) if rev else jnp.concatenate([uc, ul], axis=0)
            y = jnp.dot(u, t_ref[d, g], preferred_element_type=F32)
            x = jnp.dot(u, in_ref[d, g], preferred_element_type=F32)
            sc = sc_ref[d, g]
            for k in range(n_steps):
                if nb * 2 ** k >= x.shape[0]:
                    break
                xs = shift(x, nb * 2 ** k, rev)
                x = x + sc[k:k + 1] * xs + sc[n_steps + k:n_steps + k + 1] * pltpu.roll(xs, S5_STATE, axis=1)
            h_in = shift(x, nb, rev)
            ys.append(y + jnp.dot(h_in.astype(BF16), out_ref[d, g], preferred_element_type=F32))
        yl_ref[g] = ys[0][n_c:] + ys[1][:n_l]
        yc_ref[g] = ys[0][:n_c] + ys[1][n_l:]
        return carry
    lax.fori_loop(0, ul_ref.shape[0], group, 0)


def _s5_readout_kernel(y_ref, u_ref, d_ref, w_ref, o_ref):
    z = jax.nn.gelu(y_ref[...] + d_ref[...] * u_ref[...])
    o_ref[...] = (z * jax.nn.sigmoid(jnp.dot(z.astype(BF16), w_ref[...], preferred_element_type=F32))).astype(o_ref.dtype)


def _s5_mixer_pallas(proj, p, geo, rows):
    n_lat, seq, bsz, n_ctx = geo
    w = BRANCH_W
    nl, nc = seq // S5_CHUNK, (n_ctx // bsz) // S5_CHUNK
    n_steps = max(1, (nl + nc - 1).bit_length())
    t_op, in_op, out_op, scan_op = _s5_operators(p, nl + nc)

    def to_rows(u, n):
        u = u.reshape(bsz, n, S5_CHUNK, S5_GROUPS, S5_GROUP)
        return jnp.transpose(u, (3, 1, 0, 2, 4)).reshape(S5_GROUPS, n * bsz, S5_ROW)

    def from_rows(y, n):
        y = y.reshape(S5_GROUPS, n, bsz, S5_CHUNK, S5_GROUP)
        return jnp.transpose(y, (2, 1, 3, 0, 4)).reshape(bsz * n * S5_CHUNK, w)
    u = proj[:, :w]
    ul, uc = to_rows(u[:n_lat].astype(BF16), nl), to_rows(u[n_lat:].astype(BF16), nc)
    gb = S5_GBLK
    yl, yc = pl.pallas_call(
        functools.partial(_s5_kernel, bsz, n_steps),
        grid=(S5_GROUPS // gb,),
        in_specs=[pl.BlockSpec((gb, nl * bsz, S5_ROW), lambda i: (i, 0, 0)),
                  pl.BlockSpec((gb, nc * bsz, S5_ROW), lambda i: (i, 0, 0)),
                  pl.BlockSpec((2, gb, S5_ROW, S5_ROW), lambda i: (0, i, 0, 0)),
                  pl.BlockSpec((2, gb, S5_ROW, 2 * S5_STATE), lambda i: (0, i, 0, 0)),
                  pl.BlockSpec((2, gb, 2 * S5_STATE, S5_ROW), lambda i: (0, i, 0, 0)),
                  pl.BlockSpec((2, gb, 2 * n_steps, 2 * S5_STATE), lambda i: (0, i, 0, 0))],
        out_specs=[pl.BlockSpec((gb, nl * bsz, S5_ROW), lambda i: (i, 0, 0)),
                   pl.BlockSpec((gb, nc * bsz, S5_ROW), lambda i: (i, 0, 0))],
        out_shape=[jax.ShapeDtypeStruct((S5_GROUPS, nl * bsz, S5_ROW), F32),
                   jax.ShapeDtypeStruct((S5_GROUPS, nc * bsz, S5_ROW), F32)],
        compiler_params=_params(("parallel",)),
        name="s5_scan",
    )(ul, uc, t_op, in_op, out_op, scan_op)
    y = from_rows(yl, nl) if rows == n_lat else jnp.concatenate([from_rows(yl, nl), from_rows(yc, nc)], axis=0)
    tm = _row_tile(geo, 256)
    return pl.pallas_call(
        _s5_readout_kernel,
        grid=(rows // tm,),
        in_specs=[pl.BlockSpec((tm, w), lambda i: (i, 0)),
                  pl.BlockSpec((tm, w), lambda i: (i, 0)),
                  pl.BlockSpec((1, w), lambda i: (0, 0)),
                  pl.BlockSpec((w, w), lambda i: (0, 0))],
        out_specs=pl.BlockSpec((tm, w), lambda i: (i, 0)),
        out_shape=jax.ShapeDtypeStruct((rows, w), BF16),
        compiler_params=_params(("parallel",)),
        name="s5_readout",
    )(y, proj, p['s5_d'].reshape(1, w), p['s5_w_glu'].astype(BF16))


def _rope_tables(seq, blk):
    t = jnp.arange(seq)
    n_freq = DA_HEAD_DIM // 4
    inv = ROPE_BASE ** (-jnp.arange(n_freq, dtype=F32) / n_freq)
    ang = jnp.concatenate([(t // GRID_W).astype(F32)[:, None] * inv, (t % GRID_W).astype(F32)[:, None] * inv], axis=-1)
    cos = jnp.concatenate([jnp.cos(ang), jnp.cos(ang)], axis=-1)
    sin = jnp.concatenate([-jnp.sin(ang), jnp.sin(ang)], axis=-1)
    reps = BRANCH_W // DA_HEAD_DIM
    cos = jnp.concatenate([jnp.tile(cos, (1, reps)), jnp.ones((blk, BRANCH_W), F32)], axis=0)
    sin = jnp.concatenate([jnp.tile(sin, (1, reps)), jnp.zeros((blk, BRANCH_W), F32)], axis=0)
    return cos, sin


def _da_prep_kernel(q_ref, k_ref, v_ref, cos_ref, sin_ref, qo_ref, ko_ref, vo_ref):
    cos, sin = cos_ref[...], sin_ref[...]
    w = cos.shape[1]
    half = DA_HEAD_DIM // 2
    first = (lax.broadcasted_iota(jnp.int32, cos.shape, 1) % DA_HEAD_DIM) < half

    def rope(x):
        partner = jnp.where(first, pltpu.roll(x, w - half, axis=1), pltpu.roll(x, half, axis=1))
        return x * cos + partner * sin
    qo_ref[...] = (rope(q_ref[...]) * DA_HEAD_DIM ** -0.5).astype(qo_ref.dtype)
    ko_ref[...] = rope(k_ref[...]).astype(ko_ref.dtype)
    vo_ref[...] = v_ref[...].astype(vo_ref.dtype)


def _da_kernel(n_qb, q_ref, kl_ref, kc_ref, vl_ref, vc_ref, prm_ref, o_ref):
    lam = prm_ref[0:1, :]
    gain = prm_ref[1:2, :]
    nt = (((1,), (1,)), ((), ()))

    def attend(keys):
        q = q_ref[...]
        outs = []
        for m in range(2):
            lo, hi = m * DA_HEAD_DIM, (m + 1) * DA_HEAD_DIM
            s = [lax.dot_general(q[:, lo:hi], k_ref[:, lo:hi], nt, preferred_element_type=F32) for k_ref, _ in keys]
            mx = functools.reduce(jnp.maximum, [jnp.max(t, axis=-1, keepdims=True) for t in s])
            pr = [jnp.exp(t - mx) for t in s]
            den = functools.reduce(jnp.add, [jnp.sum(t, axis=-1, keepdims=True) for t in pr])
            num = functools.reduce(jnp.add, [jnp.dot(t.astype(BF16), v_ref[...], preferred_element_type=F32) for t, (_, v_ref) in zip(pr, keys)])
            outs.append(num / den)
        o = outs[0] - lam * outs[1]
        y = o * lax.rsqrt(jnp.mean(o * o, axis=-1, keepdims=True) + RMS_EPS) * gain
        o_ref[...] = y.astype(o_ref.dtype)

    @pl.when(pl.program_id(2) < n_qb)
    def _():
        attend([(kl_ref, vl_ref), (kc_ref, vc_ref)])

    @pl.when(pl.program_id(2) >= n_qb)
    def _():
        attend([(kc_ref, vc_ref)])


def _diff_attn_pallas(proj, p, lam_init, geo, rows):
    n_lat, seq, bsz, n_ctx = geo
    w = BRANCH_W
    n = proj.shape[0]
    tq = n_ctx // bsz
    n_qb = seq // tq
    cos, sin = _rope_tables(seq, tq)
    tab = lambda i: (jnp.where(i * tq < n_lat, i % n_qb, n_qb), 0)
    qr, kr, vb = pl.pallas_call(
        _da_prep_kernel,
        grid=(n // tq,),
        in_specs=[pl.BlockSpec((tq, w), lambda i: (i, 5)), pl.BlockSpec((tq, w), lambda i: (i, 6)), pl.BlockSpec((tq, w), lambda i: (i, 7)),
                  pl.BlockSpec((tq, w), tab), pl.BlockSpec((tq, w), tab)],
        out_specs=[pl.BlockSpec((tq, w), lambda i: (i, 0))] * 3,
        out_shape=[jax.ShapeDtypeStruct((n, w), BF16)] * 3,
        compiler_params=_params(("parallel",)),
        name="da_prep",
    )(proj, proj, proj, cos, sin)
    lam = jnp.exp(jnp.sum(p['da_lam_q1'] * p['da_lam_k1'])) - jnp.exp(jnp.sum(p['da_lam_q2'] * p['da_lam_k2'])) + lam_init
    prm = jnp.zeros((8, DA_V_DIM), F32).at[0].set(lam).at[1].set(p['da_subln'] * (1.0 - lam_init))
    with_ctx = rows > n_lat
    ctx_blk = n_lat // tq
    qrow = lambda b, h, i: (jnp.where(i < n_qb, b * n_qb + i, ctx_blk + b), h)
    return pl.pallas_call(
        functools.partial(_da_kernel, n_qb),
        grid=(bsz, DA_HEADS, n_qb + (1 if with_ctx else 0)),
        in_specs=[pl.BlockSpec((tq, DA_V_DIM), qrow),
                  pl.BlockSpec((seq, DA_V_DIM), lambda b, h, i: (b, h)),
                  pl.BlockSpec((tq, DA_V_DIM), lambda b, h, i: (ctx_blk + b, h)),
                  pl.BlockSpec((seq, DA_V_DIM), lambda b, h, i: (b, h)),
                  pl.BlockSpec((tq, DA_V_DIM), lambda b, h, i: (ctx_blk + b, h)),
                  pl.BlockSpec((8, DA_V_DIM), lambda b, h, i: (0, 0))],
        out_specs=pl.BlockSpec((tq, DA_V_DIM), qrow),
        out_shape=jax.ShapeDtypeStruct((rows, w), BF16),
        compiler_params=_params(("parallel", "parallel", "arbitrary")),
        name="diff_attn",
    )(qr, kr, kr, vb, vb, prm)


LRU_CBLK = 256
SUBLANES = 8


def _lru_kernel(xl_ref, xc_ref, gl_ref, gc_ref, cw_ref, cb_ref, wr_ref, br_ref, wi_ref, bi_ref, lam_ref, ol_ref, oc_ref, a_sc, b_sc, h_sc):
    n_l, n_c = xl_ref.shape[0], xc_ref.shape[0]
    n_all = n_l + n_c
    cw, cb = cw_ref[...], cb_ref[...]

    def conv(x):
        n = x.shape[0]
        row = lax.broadcasted_iota(jnp.int32, x.shape, 0)
        xm2 = jnp.where(row >= 2, pltpu.roll(x, 2, axis=0), 0.0)
        xm1 = jnp.where(row >= 1, pltpu.roll(x, 1, axis=0), 0.0)
        xp1 = jnp.where(row < n - 1, pltpu.roll(x, n - 1, axis=0), 0.0)
        return cw[0:1] * xm2 + cw[1:2] * xm1 + cw[2:3] * x + cw[3:4] * xp1 + cb
    xc = jnp.concatenate([conv(xc_ref[...]), conv(xl_ref[...])], axis=0)
    xb = xc.astype(BF16)
    sub = lax.broadcasted_iota(jnp.int32, xc.shape, 0) % SUBLANES
    n_blk = xc.shape[1] // LRU_BLOCK_W

    def gate(w_ref, b_ref, d):
        cols = [jnp.dot(xb[:, k * LRU_BLOCK_W:(k + 1) * LRU_BLOCK_W], w_ref[d, k], preferred_element_type=F32) for k in range(n_blk)]
        return jax.nn.sigmoid(jnp.concatenate(cols, axis=1) + b_ref[d])

    for d in range(2):
        rev = d == 1
        r, i = gate(wr_ref, br_ref, d), gate(wi_ref, bi_ref, d)
        log_a = -LRU_C * r * jax.nn.softplus(-lam_ref[d])
        a = jnp.exp(log_a)
        b = jnp.sqrt(1.0 - a * a) * (i * xc)
        for s in (1, 2, 4):
            if rev:
                keep = sub < SUBLANES - s
                a_n = jnp.where(keep, pltpu.roll(a, n_all - s, axis=0), 1.0)
                b_n = jnp.where(keep, pltpu.roll(b, n_all - s, axis=0), 0.0)
            else:
                keep = sub >= s
                a_n = jnp.where(keep, pltpu.roll(a, s, axis=0), 1.0)
                b_n = jnp.where(keep, pltpu.roll(b, s, axis=0), 0.0)
            b = b + a * b_n
            a = a * a_n
        a_sc[...] = a
        b_sc[...] = b
        last = 0 if rev else SUBLANES - 1

        def tile_step(t, carry):
            rows = pl.ds(pl.multiple_of(t * SUBLANES, SUBLANES), SUBLANES)
            h = b_sc[rows, :] + a_sc[rows, :] * carry
            if rev:
                h_sc[rows, :] += h
            else:
                h_sc[rows, :] = h
            return jnp.broadcast_to(h[last:last + 1], h.shape)
        carry = jnp.zeros((SUBLANES, xc.shape[1]), F32)
        tc, ta = n_c // SUBLANES, n_all // SUBLANES
        if rev:
            carry = lax.fori_loop(0, tc, lambda t, c: tile_step(tc - 1 - t, c), carry)
            lax.fori_loop(0, ta - tc, lambda t, c: tile_step(ta - 1 - t, c), carry)
        else:
            lax.fori_loop(0, ta, tile_step, carry)
    oc_ref[...] = (h_sc[0:n_c, :] * jax.nn.gelu(gc_ref[...])).astype(oc_ref.dtype)
    ol_ref[...] = (h_sc[n_c:n_all, :] * jax.nn.gelu(gl_ref[...])).astype(ol_ref.dtype)


def _rglru_pallas(proj, p, geo):
    n_lat, seq, bsz, n_ctx = geo
    w = BRANCH_W
    ctx_len = n_ctx // bsz
    cblk = LRU_CBLK
    ncb = w // cblk
    gpb = cblk // LRU_BLOCK_W
    lat_rows, ctx_rows = seq // ctx_len, n_lat // ctx_len
    xcol, gcol = 8 * ncb, 9 * ncb
    vec = lambda t: t.reshape(2, 1, w)
    return pl.pallas_call(
        _lru_kernel,
        grid=(bsz, ncb),
        in_specs=[pl.BlockSpec((seq, cblk), lambda b, c: (b, xcol + c)),
                  pl.BlockSpec((ctx_len, cblk), lambda b, c: (ctx_rows + b, xcol + c)),
                  pl.BlockSpec((seq, cblk), lambda b, c: (b, gcol + c)),
                  pl.BlockSpec((ctx_len, cblk), lambda b, c: (ctx_rows + b, gcol + c)),
                  pl.BlockSpec((4, cblk), lambda b, c: (0, c)),
                  pl.BlockSpec((1, cblk), lambda b, c: (0, c)),
                  pl.BlockSpec((2, gpb, LRU_BLOCK_W, LRU_BLOCK_W), lambda b, c: (0, c, 0, 0)),
                  pl.BlockSpec((2, 1, cblk), lambda b, c: (0, 0, c)),
                  pl.BlockSpec((2, gpb, LRU_BLOCK_W, LRU_BLOCK_W), lambda b, c: (0, c, 0, 0)),
                  pl.BlockSpec((2, 1, cblk), lambda b, c: (0, 0, c)),
                  pl.BlockSpec((2, 1, cblk), lambda b, c: (0, 0, c))],
        out_specs=[pl.BlockSpec((seq, cblk), lambda b, c: (b, c)),
                   pl.BlockSpec((ctx_len, cblk), lambda b, c: (b, c))],
        out_shape=[jax.ShapeDtypeStruct((n_lat, w), BF16), jax.ShapeDtypeStruct((n_ctx, w), BF16)],
        scratch_shapes=[pltpu.VMEM((seq + ctx_len, cblk), F32)] * 3,
        compiler_params=_params(("parallel", "parallel")),
        name="rglru",
    )(proj, proj, proj, proj, p['lru_conv_w'], p['lru_conv_b'].reshape(1, w), p['lru_w_rg'].astype(BF16), vec(p['lru_b_rg']),
      p['lru_w_ig'].astype(BF16), vec(p['lru_b_ig']), vec(p['lru_lam']))


DN_PREP_CBLK = 512
DN_HPB = 2
DN_DOUBLINGS = 5


def _dn_prep_kernel(xl_ref, xc_ref, cw_ref, ol_ref, oc_ref):
    cw = cw_ref[...]
    cblk = cw.shape[1]
    cb = pl.program_id(1)
    n_q = BRANCH_W // cblk

    def prep(x):
        n = x.shape[0]
        row = lax.broadcasted_iota(jnp.int32, x.shape, 0)
        xm2 = jnp.where(row >= 2, pltpu.roll(x, 2, axis=0), 0.0)
        xm1 = jnp.where(row >= 1, pltpu.roll(x, 1, axis=0), 0.0)
        xp1 = jnp.where(row < n - 1, pltpu.roll(x, n - 1, axis=0), 0.0)
        y = cw[0:1] * xm2 + cw[1:2] * xm1 + cw[2:3] * x + cw[3:4] * xp1
        y = y * jax.nn.sigmoid(y)
        heads = [y[:, h * DN_HEAD_DIM:(h + 1) * DN_HEAD_DIM] for h in range(cblk // DN_HEAD_DIM)]
        normed = jnp.concatenate([t * lax.rsqrt(jnp.sum(t * t, axis=-1, keepdims=True) + 1e-6) for t in heads], axis=1)
        scale = jnp.where(cb < n_q, DN_HEAD_DIM ** -0.5, 1.0)
        return jnp.where(cb < 2 * n_q, normed * scale, y)
    ol_ref[...] = prep(xl_ref[...]).astype(ol_ref.dtype)
    oc_ref[...] = prep(xc_ref[...]).astype(oc_ref.dtype)


def _dn_kernel(ql_ref, qc_ref, kl_ref, kc_ref, vl_ref, vc_ref, zl_ref, zc_ref, abl_ref, abc_ref, prm_ref, ol_ref, oc_ref,
               q_sc, k_sc, v_sc, g_sc, b_sc, o_sc):
    n_l, n_c = ql_ref.shape[0], qc_ref.shape[0]
    n_all = n_l + n_c
    c = DN_CHUNK
    hd = DN_HEAD_DIM
    ncc, nch = n_c // c, n_all // c
    hp = pl.program_id(1)
    for sc, rc, rl in ((q_sc, qc_ref, ql_ref), (k_sc, kc_ref, kl_ref), (v_sc, vc_ref, vl_ref)):
        sc[0:n_c, :] = rc[...]
        sc[n_c:n_all, :] = rl[...]
    ab = jnp.concatenate([abc_ref[...], abl_ref[...]], axis=0)
    g_all = prm_ref[0:1, :] * jax.nn.softplus(ab + prm_ref[1:2, :])
    beta_all = jax.nn.sigmoid(ab)
    lane = lax.broadcasted_iota(jnp.int32, ab.shape, 1)
    pos = lax.broadcasted_iota(jnp.int32, ab.shape, 0) % c
    for d in range(2):
        for s in range(DN_HPB):
            col = d * DN_HEADS + hp * DN_HPB + s
            g = jnp.broadcast_to(jnp.sum(jnp.where(lane == col, g_all, 0.0), axis=-1, keepdims=True), ab.shape)
            step = 1
            while step < c:
                if d == 0:
                    g = g + jnp.where(pos >= step, pltpu.roll(g, step, axis=0), 0.0)
                else:
                    g = g + jnp.where(pos < c - step, pltpu.roll(g, n_all - step, axis=0), 0.0)
                step *= 2
            g_sc[d, s] = g
            b_sc[d, s] = jnp.broadcast_to(jnp.sum(jnp.where(lane == 2 * DN_HEADS + col, beta_all, 0.0), axis=-1, keepdims=True), ab.shape)
    ri = lax.broadcasted_iota(jnp.int32, (c, c), 0)
    ci = lax.broadcasted_iota(jnp.int32, (c, c), 1)
    eye = (ri == ci).astype(F32)
    incl = (ri >= ci, ri <= ci)
    strict = (ri > ci, ri < ci)
    nt = (((1,), (1,)), ((), ()))

    def chunk(d, s, n, state):
        rows = pl.ds(pl.multiple_of(n * c, c), c)
        hs = slice(s * hd, (s + 1) * hd)
        q, k, v = q_sc[rows, hs], k_sc[rows, hs], v_sc[rows, hs]
        gc, beta = g_sc[d, s, rows, :], b_sc[d, s, rows, :]
        qf, kf, vf = q.astype(F32), k.astype(F32), v.astype(F32)
        diff = gc[:, :c] - gc.T[:c, :]
        decay = jnp.exp(jnp.where(incl[d], diff, -jnp.inf))
        kb = kf * beta
        a = jnp.where(strict[d], lax.dot_general(kb.astype(BF16), k, nt, preferred_element_type=F32), 0.0) * decay
        x, pw = eye - a, a
        for _ in range(DN_DOUBLINGS):
            pb = pw.astype(BF16)
            pw = jnp.dot(pb, pb, preferred_element_type=F32)
            x = x + jnp.dot(x.astype(BF16), pw.astype(BF16), preferred_element_type=F32)
        eg = jnp.exp(gc)
        uw = jnp.dot(x.astype(BF16), jnp.concatenate([vf * beta, kb * eg], axis=1).astype(BF16), preferred_element_type=F32)
        u, w = uw[:, :hd], uw[:, hd:]
        qk = lax.dot_general(q, k, nt, preferred_element_type=F32) * decay
        sb = state.astype(BF16)
        v_new = u - jnp.dot(w.astype(BF16), sb, preferred_element_type=F32)
        vb = v_new.astype(BF16)
        o = jnp.dot((qf * eg).astype(BF16), sb, preferred_element_type=F32) + jnp.dot(qk.astype(BF16), vb, preferred_element_type=F32)
        g_last = gc[c - 1:c] if d == 0 else gc[0:1]
        k_dec = kf * jnp.exp(g_last - gc)
        o_sc[d, rows, hs] = o
        return state * jnp.exp(g_last) + jnp.dot(k_dec.T.astype(BF16), vb, preferred_element_type=F32)

    def body(t, states):
        n_rev = jnp.where(t < ncc, ncc - 1 - t, nch - 1 - (t - ncc))
        return tuple(chunk(d, s, t if d == 0 else n_rev, states[d * DN_HPB + s]) for d in range(2) for s in range(DN_HPB))
    lax.fori_loop(0, nch, body, tuple(jnp.zeros((hd, hd), F32) for _ in range(2 * DN_HPB)))

    gain = prm_ref[2:3, :]

    def readout(o, z):
        cols = []
        for s in range(DN_HPB):
            t, zz = o[:, s * hd:(s + 1) * hd], z[:, s * hd:(s + 1) * hd]
            cols.append(t * lax.rsqrt(jnp.mean(t * t, axis=-1, keepdims=True) + RMS_EPS) * gain * (zz * jax.nn.sigmoid(zz)))
        return jnp.concatenate(cols, axis=1)
    oc_ref[...] = readout(o_sc[0, 0:n_c, :] + o_sc[1, 0:n_c, :], zc_ref[...]).astype(oc_ref.dtype)
    ol_ref[...] = readout(o_sc[0, n_c:n_all, :] + o_sc[1, n_c:n_all, :], zl_ref[...]).astype(ol_ref.dtype)


def _deltanet_pallas(proj, p, geo):
    n_lat, seq, bsz, n_ctx = geo
    w = BRANCH_W
    ctx_len = n_ctx // bsz
    ctx_rows = n_lat // ctx_len
    cblk = DN_PREP_CBLK
    c0 = w // cblk
    qkv_l, qkv_c = pl.pallas_call(
        _dn_prep_kernel,
        grid=(bsz, 3 * w // cblk),
        in_specs=[pl.BlockSpec((seq, cblk), lambda b, j: (b, c0 + j)),
                  pl.BlockSpec((ctx_len, cblk), lambda b, j: (ctx_rows + b, c0 + j)),
                  pl.BlockSpec((4, cblk), lambda b, j: (0, j))],
        out_specs=[pl.BlockSpec((seq, cblk), lambda b, j: (b, j)),
                   pl.BlockSpec((ctx_len, cblk), lambda b, j: (b, j))],
        out_shape=[jax.ShapeDtypeStruct((n_lat, 3 * w), BF16), jax.ShapeDtypeStruct((n_ctx, 3 * w), BF16)],
        compiler_params=_params(("parallel", "parallel")),
        name="dn_prep",
    )(proj, proj, p['dn_conv_w'])
    hw = DN_HPB * DN_HEAD_DIM
    nhb = w // hw
    zcol, abcol = 4 * w // hw, 10 * w // LANES
    n16 = 2 * DN_HEADS
    prm = (jnp.zeros((8, LANES), F32).at[0, :n16].set(-jnp.exp(p['dn_a_log']).reshape(n16))
           .at[1, :n16].set(p['dn_dt_bias'].reshape(n16)).at[2, :DN_HEAD_DIM].set(p['dn_norm']))
    lat = lambda off: pl.BlockSpec((seq, hw), lambda b, h: (b, off + h))
    ctxs = lambda off: pl.BlockSpec((ctx_len, hw), lambda b, h: (b, off + h))
    n_all = seq + ctx_len
    return pl.pallas_call(
        _dn_kernel,
        grid=(bsz, nhb),
        in_specs=[lat(0), ctxs(0), lat(nhb), ctxs(nhb), lat(2 * nhb), ctxs(2 * nhb),
                  pl.BlockSpec((seq, hw), lambda b, h: (b, zcol + h)),
                  pl.BlockSpec((ctx_len, hw), lambda b, h: (ctx_rows + b, zcol + h)),
                  pl.BlockSpec((seq, LANES), lambda b, h: (b, abcol)),
                  pl.BlockSpec((ctx_len, LANES), lambda b, h: (ctx_rows + b, abcol)),
                  pl.BlockSpec((8, LANES), lambda b, h: (0, 0))],
        out_specs=[pl.BlockSpec((seq, hw), lambda b, h: (b, h)),
                   pl.BlockSpec((ctx_len, hw), lambda b, h: (b, h))],
        out_shape=[jax.ShapeDtypeStruct((n_lat, w), BF16), jax.ShapeDtypeStruct((n_ctx, w), BF16)],
        scratch_shapes=[pltpu.VMEM((n_all, hw), BF16)] * 3
        + [pltpu.VMEM((2, DN_HPB, n_all, LANES), F32)] * 2
        + [pltpu.VMEM((2, n_all, hw), F32)],
        compiler_params=_params(("parallel", "parallel")),
        name="deltanet",
    )(qkv_l, qkv_c, qkv_l, qkv_c, qkv_l, qkv_c, proj, proj, proj, proj, prm)


def _mixers(proj, p, lam_init, geo, rows):
    n_lat = geo[0]
    with_ctx = rows > n_lat
    dn_l, dn_c = _deltanet_pallas(proj, p, geo)
    lru_l, lru_c = _rglru_pallas(proj, p, geo)

    def join(lat, ctx_part):
        return jnp.concatenate([lat, ctx_part], axis=0) if with_ctx else lat
    return [_s5_mixer_pallas(proj, p, geo, rows), join(dn_l, dn_c), _diff_attn_pallas(proj, p, lam_init, geo, rows), join(lru_l, lru_c)]


def kernel(x, c, ctx, c_ctx, ada_w, ada_b, mix_norm, ffn_norm, w_in, s5_lam_re, s5_lam_im, s5_log_dt, s5_b_re, s5_b_im, s5_c_re, s5_c_im, s5_d, s5_w_glu, dn_conv_w, dn_a_log, dn_dt_bias, dn_norm, da_lam_q1, da_lam_k1, da_lam_q2, da_lam_k2, da_subln, lru_conv_w, lru_conv_b, lru_w_rg, lru_b_rg, lru_w_ig, lru_b_ig, lru_lam, w_branch, w_out, router_w, router_b, moe_w_gu, moe_b_gu, moe_w_down, moe_b_down, final_norm):
    bsz, seq, d = x.shape
    n_ctx_tok = ctx.shape[1]
    depth = ada_w.shape[0]
    n_ctx = bsz * n_ctx_tok
    n_lat = bsz * seq
    n_tok = n_ctx + n_lat
    geo = (n_lat, seq, bsz, n_ctx)
    w = BRANCH_W
    mix_cols = 10 * w + 4 * DN_HEADS
    ab_lo = 5 * w

    xs = jnp.concatenate([x.reshape(n_lat, d), ctx.reshape(n_ctx, d)], axis=0)
    mods = _ada_table(c, c_ctx, ada_w, ada_b)

    for l in range(depth):
        with_ctx = l < depth - 1
        lam_init = 0.8 - 0.6 * math.exp(-0.3 * l)
        rows = n_tok if with_ctx else n_lat
        mod = mods[l]
        p = {'s5_lam_re': s5_lam_re[l], 's5_lam_im': s5_lam_im[l], 's5_log_dt': s5_log_dt[l], 's5_b_re': s5_b_re[l], 's5_b_im': s5_b_im[l], 's5_c_re': s5_c_re[l], 's5_c_im': s5_c_im[l], 's5_d': s5_d[l], 's5_w_glu': s5_w_glu[l], 'dn_conv_w': dn_conv_w[l], 'dn_a_log': dn_a_log[l], 'dn_dt_bias': dn_dt_bias[l], 'dn_norm': dn_norm[l], 'da_lam_q1': da_lam_q1[l], 'da_lam_k1': da_lam_k1[l], 'da_lam_q2': da_lam_q2[l], 'da_lam_k2': da_lam_k2[l], 'da_subln': da_subln[l], 'lru_conv_w': lru_conv_w[l], 'lru_conv_b': lru_conv_b[l], 'lru_w_rg': lru_w_rg[l], 'lru_b_rg': lru_b_rg[l], 'lru_w_ig': lru_w_ig[l], 'lru_b_ig': lru_b_ig[l], 'lru_lam': lru_lam[l]}
        wl = w_in[l]
        w_mix = jnp.concatenate([wl[:, :ab_lo], wl[:, ab_lo + 4 * DN_HEADS:mix_cols], wl[:, ab_lo:ab_lo + 4 * DN_HEADS],
                                 jnp.zeros((d, AB_COLS - 4 * DN_HEADS), F32)], axis=1).astype(BF16)
        w_gate = wl[:, mix_cols:].astype(BF16)
        wgu = moe_w_gu[l].astype(BF16)
        bgu = moe_b_gu[l][:, None, :]
        wd = moe_w_down[l].astype(BF16)
        wd = jnp.stack([wd, jnp.zeros_like(wd)], axis=2).reshape(wd.shape[0], 2 * wd.shape[1], d)
        bd = moe_b_down[l][:, None, :]

        h = _norm_mod(xs, mix_norm[l], mod, geo)
        proj = _matmul(h, w_mix, F32)
        branches = _mixers(proj, p, lam_init, geo, rows)
        merged = _merge(h, branches, w_gate, w_branch[l].astype(BF16), rows)
        xs = _outproj_residual(merged, w_out[l].astype(BF16), xs, mod, geo, rows)
        final = l == depth - 1
        xs = _moe_layer(xs, ffn_norm[l], mod, router_w[l], router_b[l], wgu, bgu, wd, bd, final_norm, geo, rows, final)
    return xs.reshape(bsz, seq, d)
```

```python
import functools
import math

import jax
import jax.numpy as jnp
from jax import lax
from jax.experimental import pallas as pl
from jax.experimental.pallas import tpu as pltpu

F32 = jnp.float32
BF16 = jnp.bfloat16

GRID_W = 64
N_BRANCH = 4
BRANCH_W = 1024
RMS_EPS = 1e-6
N_MOD = 6
ROPE_BASE = 10000.0
S5_GROUP = 16
S5_GROUPS = BRANCH_W // S5_GROUP
S5_STATE = 64
DN_HEADS = 8
DN_HEAD_DIM = BRANCH_W // DN_HEADS
DN_CHUNK = 64
DA_HEADS = 8
DA_HEAD_DIM = BRANCH_W // (2 * DA_HEADS)
DA_V_DIM = 2 * DA_HEAD_DIM
Q_BLOCK = 128
LRU_BLOCKS = 8
LRU_BLOCK_W = BRANCH_W // LRU_BLOCKS
LRU_C = 8.0
TOP_K = 4
SWIGLU_LIMIT = 7.0
SWIGLU_ALPHA = 1.702

MOD_ROWS = 8
LANES = 128
MOE_TILE = 256
AB_COLS = 128
VMEM_LIMIT = 56 << 20


def _params(sem, vmem=VMEM_LIMIT):
    return pltpu.CompilerParams(dimension_semantics=sem, vmem_limit_bytes=vmem)


def _tile(n, pref):
    t = min(n, pref)
    while n % t:
        t //= 2
    return t


def _ada_kernel(c_ref, w_ref, b_ref, o_ref):
    @pl.when(pl.program_id(2) == 0)
    def _():
        o_ref[0] = jnp.broadcast_to(b_ref[0], o_ref.shape[1:])
    c = c_ref[...]
    s = (c * jax.nn.sigmoid(c)).astype(BF16)
    o_ref[0] += jnp.dot(s, w_ref[0].astype(BF16), preferred_element_type=F32)


def _ada_table(c, c_ctx, ada_w, ada_b):
    depth, d, nm = ada_w.shape
    bsz = c.shape[0]
    cs = jnp.zeros((MOD_ROWS, d), F32).at[:bsz].set(c).at[bsz].set(c_ctx)
    tk, tn = _tile(d, 1024), _tile(nm, 2048)
    mod = pl.pallas_call(
        _ada_kernel,
        grid=(depth, nm // tn, d // tk),
        in_specs=[pl.BlockSpec((MOD_ROWS, tk), lambda l, n, k: (0, k)),
                  pl.BlockSpec((1, tk, tn), lambda l, n, k: (l, k, n)),
                  pl.BlockSpec((1, 1, tn), lambda l, n, k: (l, 0, n))],
        out_specs=pl.BlockSpec((1, MOD_ROWS, tn), lambda l, n, k: (l, 0, n)),
        out_shape=jax.ShapeDtypeStruct((depth, MOD_ROWS, nm), F32),
        compiler_params=_params(("parallel", "parallel", "arbitrary")),
        name="ada_table",
    )(cs, ada_w, ada_b.reshape(depth, 1, nm))
    return jnp.moveaxis(mod.reshape(depth, MOD_ROWS, N_MOD, d), 1, 2).reshape(depth, N_MOD * MOD_ROWS, 1, d)


def _mod_index(which, tm, geo):
    n_lat, seq, bsz, _ = geo

    def idx(i):
        r = i * tm
        return which * MOD_ROWS + jnp.where(r < n_lat, r // seq, bsz)
    return idx


def _row_tile(geo, pref):
    return _tile(math.gcd(geo[1], geo[3]), pref)


def _norm_mod_kernel(x_ref, g_ref, sh_ref, sc_ref, o_ref):
    x = x_ref[...]
    y = x * lax.rsqrt(jnp.mean(x * x, axis=-1, keepdims=True) + RMS_EPS) * g_ref[...]
    o_ref[...] = (y * (1.0 + sc_ref[0]) + sh_ref[0]).astype(o_ref.dtype)


def _norm_mod(x, gain, mod, geo):
    n, d = x.shape
    tm = _row_tile(geo,256)
    sh, sc = _mod_index(0, tm, geo), _mod_index(1, tm, geo)
    return pl.pallas_call(
        _norm_mod_kernel,
        grid=(n // tm,),
        in_specs=[pl.BlockSpec((tm, d), lambda i: (i, 0)),
                  pl.BlockSpec((1, d), lambda i: (0, 0)),
                  pl.BlockSpec((1, 1, d), lambda i: (sh(i), 0, 0)),
                  pl.BlockSpec((1, 1, d), lambda i: (sc(i), 0, 0))],
        out_specs=pl.BlockSpec((tm, d), lambda i: (i, 0)),
        out_shape=jax.ShapeDtypeStruct((n, d), BF16),
        compiler_params=_params(("parallel",)),
        name="norm_mod",
    )(x, gain.reshape(1, d), mod, mod)


def _mm_kernel(a_ref, b_ref, o_ref):
    o_ref[...] = jnp.dot(a_ref[...], b_ref[...], preferred_element_type=F32).astype(o_ref.dtype)


def _matmul(a, b, out_dtype, tm_pref=512, tn_pref=1152):
    m, k = a.shape
    n = b.shape[1]
    tm = _tile(m, tm_pref)
    tn = tn_pref if n % tn_pref == 0 else _tile(n, 1024)
    return pl.pallas_call(
        _mm_kernel,
        grid=(m // tm, n // tn),
        in_specs=[pl.BlockSpec((tm, k), lambda i, j: (i, 0)),
                  pl.BlockSpec((k, tn), lambda i, j: (0, j))],
        out_specs=pl.BlockSpec((tm, tn), lambda i, j: (i, j)),
        out_shape=jax.ShapeDtypeStruct((m, n), out_dtype),
        compiler_params=_params(("parallel", "parallel")),
        name="in_proj",
    )(a, b)


def _merge_kernel(h_ref, b0, b1, b2, b3, g0, g1, g2, g3, wb_ref, o_ref):
    h = h_ref[...]
    acc = None
    for k, (br, wg) in enumerate(((b0, g0), (b1, g1), (b2, g2), (b3, g3))):
        gate = jax.nn.sigmoid(jnp.dot(h, wg[...], preferred_element_type=F32))
        lifted = jnp.dot(br[...], wb_ref[k], preferred_element_type=F32)
        acc = gate * lifted if acc is None else acc + gate * lifted
    o_ref[...] = acc.astype(o_ref.dtype)


def _merge(h, branches, w_gate, w_branch, rows):
    d = h.shape[1]
    w = branches[0].shape[1]
    tm, tn = _tile(rows, 512), _tile(d, 256)
    nt = d // tn
    gate_specs = [pl.BlockSpec((d, tn), functools.partial(lambda i, j, k: (0, k * nt + j), k=k)) for k in range(N_BRANCH)]
    return pl.pallas_call(
        _merge_kernel,
        grid=(rows // tm, nt),
        in_specs=[pl.BlockSpec((tm, d), lambda i, j: (i,0))]
        + [pl.BlockSpec((tm, w), lambda i, j: (i,0))] * N_BRANCH
        + gate_specs
        + [pl.BlockSpec((N_BRANCH, w, tn), lambda i, j: (0, 0, j))],
        out_specs=pl.BlockSpec((tm, tn), lambda i, j: (i, j)),
        out_shape=jax.ShapeDtypeStruct((rows, d), BF16),
        compiler_params=_params(("parallel", "parallel")),
        name="merge",
    )(h, *branches, w_gate, w_gate, w_gate, w_gate, w_branch)


def _outproj_kernel(m_ref, w_ref, x_ref, g_ref, o_ref):
    o_ref[...] = x_ref[...] + g_ref[0] * jnp.dot(m_ref[...], w_ref[...], preferred_element_type=F32)


def _outproj_residual(merged, w_out, x, mod, geo, rows):
    n, d = x.shape
    tm, tn = _row_tile(geo,512), _tile(d, 1024)
    gi = _mod_index(2, tm, geo)
    return pl.pallas_call(
        _outproj_kernel,
        grid=(rows // tm, d // tn),
        in_specs=[pl.BlockSpec((tm, d), lambda i, j: (i, 0)),
                  pl.BlockSpec((d, tn), lambda i, j: (0, j)),
                  pl.BlockSpec((tm, tn), lambda i, j: (i,j)),
                  pl.BlockSpec((1, 1, tn), lambda i, j: (gi(i), 0, j))],
        out_specs=pl.BlockSpec((tm, tn), lambda i, j: (i, j)),
        out_shape=jax.ShapeDtypeStruct((rows, d), F32),
        compiler_params=_params(("parallel", "parallel")),
        name="out_proj",
    )(merged, w_out, x, mod)


def _router_kernel(n_experts, x_ref, g_ref, sh_ref, sc_ref, rw_ref, rb_ref, h_ref, ti_ref, tw_ref):
    x = x_ref[...]
    y = x * lax.rsqrt(jnp.mean(x * x, axis=-1, keepdims=True) + RMS_EPS) * g_ref[...]
    h = y * (1.0 + sc_ref[0]) + sh_ref[0]
    h_ref[...] = h
    logits = jnp.dot(h.astype(BF16), rw_ref[...], preferred_element_type=F32) + rb_ref[...]
    lane = lax.broadcasted_iota(jnp.int32, logits.shape, 1)
    logits = jnp.where(lane < n_experts, logits, -jnp.inf)
    idx_out = jnp.zeros(logits.shape, jnp.int32)
    val_out = jnp.zeros(logits.shape, F32)
    top = None
    denom = None
    for k in range(TOP_K):
        m = jnp.max(logits, axis=-1, keepdims=True)
        idx = jnp.min(jnp.where(logits == m, lane, LANES), axis=-1, keepdims=True)
        top = m if top is None else top
        e = jnp.exp(m - top)
        denom = e if denom is None else denom + e
        idx_out = jnp.where(lane == k, idx, idx_out)
        val_out = jnp.where(lane == k, e, val_out)
        logits = jnp.where(lane == idx, -jnp.inf, logits)
    ti_ref[...] = idx_out
    tw_ref[...] = val_out / denom


def _router(x, gain, mod, router_w, router_b, geo, rows):
    n, d = x.shape
    n_experts = router_w.shape[1]
    tm = _row_tile(geo,256)
    sh, sc = _mod_index(3, tm, geo), _mod_index(4, tm, geo)
    rw = jnp.zeros((d, LANES), BF16).at[:, :n_experts].set(router_w.astype(BF16))
    rb = jnp.zeros((1, LANES), F32).at[0, :n_experts].set(router_b)
    return pl.pallas_call(
        functools.partial(_router_kernel, n_experts),
        grid=(rows // tm,),
        in_specs=[pl.BlockSpec((tm, d), lambda i: (i, 0)),
                  pl.BlockSpec((1, d), lambda i: (0, 0)),
                  pl.BlockSpec((1, 1, d), lambda i: (sh(i), 0, 0)),
                  pl.BlockSpec((1, 1, d), lambda i: (sc(i), 0, 0)),
                  pl.BlockSpec((d, LANES), lambda i: (0, 0)),
                  pl.BlockSpec((1, LANES), lambda i: (0, 0))],
        out_specs=[pl.BlockSpec((tm, d), lambda i: (i, 0)),
                   pl.BlockSpec((tm, LANES), lambda i: (i, 0)),
                   pl.BlockSpec((tm, LANES), lambda i: (i, 0))],
        out_shape=[jax.ShapeDtypeStruct((rows, d), F32),
                   jax.ShapeDtypeStruct((rows, LANES), jnp.int32),
                   jax.ShapeDtypeStruct((rows, LANES), F32)],
        compiler_params=_params(("parallel",)),
        name="router",
    )(x, gain.reshape(1, d), mod, mod, rw, rb)


def _row_copy(src_hbm, row, dst, r, sem):
    return pltpu.make_async_copy(src_hbm.at[pl.ds(row, 1)], dst.at[pl.ds(r, 1)], sem)


def _expert_kernel(te_ref, used_ref, src_ref, h_hbm, wgu_ref, bgu_ref, wd_ref, bd_ref, o_ref, xbuf, sem):
    g = pl.program_id(0)
    n_used = used_ref[0]
    slot = g % 2

    def issue(tile, s):
        def body(r, carry):
            _row_copy(h_hbm, src_ref[tile * MOE_TILE + r], xbuf.at[s], r, sem.at[s]).start()
            return carry
        lax.fori_loop(0, MOE_TILE, body, 0, unroll=8)

    def wait(s):
        def body(r, carry):
            _row_copy(h_hbm, 0, xbuf.at[s], r, sem.at[s]).wait()
            return carry
        lax.fori_loop(0, MOE_TILE, body, 0, unroll=8)

    @pl.when((g == 0) & (n_used > 0))
    def _():
        issue(0, 0)

    @pl.when(g + 1 < n_used)
    def _():
        issue(g + 1, 1 - slot)

    @pl.when(g < n_used)
    def _():
        wait(slot)
        x = xbuf[slot].astype(BF16)
        gu = jnp.dot(x, wgu_ref[0], preferred_element_type=F32) + bgu_ref[0]
        gate = jnp.minimum(gu, SWIGLU_LIMIT)
        up = jnp.clip(pltpu.roll(gu, gu.shape[1] - 1, axis=1), -SWIGLU_LIMIT, SWIGLU_LIMIT)
        act = (up + 1.0) * (gate * jax.nn.sigmoid(SWIGLU_ALPHA * gate))
        even = lax.broadcasted_iota(jnp.int32, act.shape, 1) % 2 == 0
        act = jnp.where(even, act, 0.0)
        o_ref[...] = jnp.dot(act.astype(BF16), wd_ref[0], preferred_element_type=F32) + bd_ref[0]

    @pl.when(g >= n_used)
    def _():
        o_ref[...] = jnp.zeros_like(o_ref)


def _expert_ffn(h, tile_expert, n_used, row_src, wgu, bgu, wd, bd):
    d = h.shape[1]
    ff2 = wgu.shape[2]
    n_tiles = tile_expert.shape[0]
    grid_spec = pltpu.PrefetchScalarGridSpec(
        num_scalar_prefetch=3,
        grid=(n_tiles,),
        in_specs=[pl.BlockSpec(memory_space=pl.ANY),
                  pl.BlockSpec((1, d, ff2), lambda g, te, nu, rs: (te[g], 0, 0)),
                  pl.BlockSpec((1, 1, ff2), lambda g, te, nu, rs: (te[g], 0, 0)),
                  pl.BlockSpec((1, ff2, d), lambda g, te, nu, rs: (te[g], 0, 0)),
                  pl.BlockSpec((1, 1, d), lambda g, te, nu, rs: (te[g], 0, 0))],
        out_specs=pl.BlockSpec((MOE_TILE, d), lambda g, te, nu, rs: (g, 0)),
        scratch_shapes=[pltpu.VMEM((2, MOE_TILE, d), F32), pltpu.SemaphoreType.DMA((2,))],
    )
    return pl.pallas_call(
        _expert_kernel,
        grid_spec=grid_spec,
        out_shape=jax.ShapeDtypeStruct((n_tiles * MOE_TILE, d), F32),
        compiler_params=_params(("arbitrary",)),
        name="expert_ffn",
    )(tile_expert, n_used, row_src, h, wgu, bgu, wd, bd)


def _combine_kernel(tm, final, pos_ref, x_ref, tw_ref, g_ref, fn_ref, ys_hbm, o_ref, ybuf, sem):
    i = pl.program_id(0)
    n_steps = pl.num_programs(0)
    slot = i % 2

    def issue(step, s):
        def body(r, carry):
            for k in range(TOP_K):
                _row_copy(ys_hbm, pos_ref[(step * tm + r) * TOP_K + k], ybuf.at[s, k], r, sem.at[s]).start()
            return carry
        lax.fori_loop(0, tm, body, 0, unroll=4)

    def wait(s):
        def body(r, carry):
            for k in range(TOP_K):
                _row_copy(ys_hbm, 0, ybuf.at[s, k], r, sem.at[s]).wait()
            return carry
        lax.fori_loop(0, tm, body, 0, unroll=4)

    @pl.when(i == 0)
    def _():
        issue(0, 0)

    @pl.when(i + 1 < n_steps)
    def _():
        issue(i + 1, 1 - slot)

    wait(slot)
    tw = tw_ref[...]
    acc = tw[:, 0:1] * ybuf[slot, 0]
    for k in range(1, TOP_K):
        acc = acc + tw[:, k:k + 1] * ybuf[slot, k]
    y = x_ref[...] + g_ref[0] * acc
    if final:
        y = y * lax.rsqrt(jnp.mean(y * y, axis=-1, keepdims=True) + RMS_EPS) * fn_ref[...]
    o_ref[...] = y


def _combine(x, pos, tw, ys, mod, final_gain, geo, rows, final):
    n, d = x.shape
    tm = _row_tile(geo,128)
    gi = _mod_index(5, tm, geo)
    grid_spec = pltpu.PrefetchScalarGridSpec(
        num_scalar_prefetch=1,
        grid=(rows // tm,),
        in_specs=[pl.BlockSpec((tm, d), lambda i, p: (i, 0)),
                  pl.BlockSpec((tm, LANES), lambda i, p: (i, 0)),
                  pl.BlockSpec((1, 1, d), lambda i, p: (gi(i), 0, 0)),
                  pl.BlockSpec((1, d), lambda i, p: (0, 0)),
                  pl.BlockSpec(memory_space=pl.ANY)],
        out_specs=pl.BlockSpec((tm, d), lambda i, p: (i, 0)),
        scratch_shapes=[pltpu.VMEM((2, TOP_K, tm, d), F32), pltpu.SemaphoreType.DMA((2,))],
    )
    return pl.pallas_call(
        functools.partial(_combine_kernel, tm, final),
        grid_spec=grid_spec,
        out_shape=jax.ShapeDtypeStruct((rows, d), F32),
        compiler_params=_params(("arbitrary",)),
        name="moe_combine",
    )(pos, x, tw, mod, final_gain.reshape(1, d), ys)


def _moe_layer(x, gain, mod, router_w, router_b, wgu, bgu, wd, bd, final_gain, geo, rows, final):
    n_exp = wgu.shape[0]
    h, top_i, top_w = _router(x, gain, mod, router_w, router_b, geo, rows)
    e_flat = top_i[:, :TOP_K].reshape(-1)
    onehot = (e_flat[:, None] == jnp.arange(n_exp, dtype=jnp.int32)[None, :]).astype(jnp.int32)
    rank = jnp.sum((jnp.cumsum(onehot, axis=0) - onehot) * onehot, axis=1)
    counts = jnp.sum(onehot, axis=0)
    padded = ((counts + MOE_TILE - 1) // MOE_TILE) * MOE_TILE
    ends = jnp.cumsum(padded)
    pos = ((ends - padded)[e_flat] + rank).astype(jnp.int32)
    n_tiles = (rows * TOP_K + n_exp * (MOE_TILE - 1)) // MOE_TILE + 1
    tok = jnp.arange(rows * TOP_K, dtype=jnp.int32) // TOP_K
    row_src = jnp.zeros((n_tiles * MOE_TILE,), jnp.int32).at[pos].set(tok, unique_indices=True)
    tile_start = jnp.arange(n_tiles, dtype=jnp.int32) * MOE_TILE
    tile_expert = jnp.minimum(jnp.searchsorted(ends, tile_start, side="right"), n_exp - 1).astype(jnp.int32)
    n_used = (ends[-1] // MOE_TILE).astype(jnp.int32).reshape(1)
    ys = _expert_ffn(h, tile_expert, n_used, row_src, wgu, bgu, wd, bd)
    return _combine(x, pos, top_w, ys, mod, final_gain, geo, rows, final)


S5_CHUNK = 16
S5_ROW = S5_CHUNK * S5_GROUP
S5_GBLK = 8


def _s5_operators(p, n_chunks):
    hi = lax.Precision.HIGHEST
    n_steps = max(1, (n_chunks - 1).bit_length())
    ci = jnp.arange(S5_CHUNK)
    t_ops, in_ops, out_ops, scan_ops = [], [], [], []
    for d in range(2):
        lam = lax.complex(p['s5_lam_re'][d], p['s5_lam_im'][d])
        dt = jnp.exp(p['s5_log_dt'][d])[:, None]
        lam_bar = jnp.exp(lam * dt)
        b_bar = ((lam_bar - 1.0) / lam)[..., None] * lax.complex(p['s5_b_re'][d], p['s5_b_im'][d])
        c_mat = lax.complex(p['s5_c_re'][d], p['s5_c_im'][d])
        pw = [jnp.ones_like(lam_bar)]
        for _ in range(S5_CHUNK):
            pw.append(pw[-1] * lam_bar)
        pw = jnp.stack(pw)
        kern = jnp.einsum('gop,mgp,gpi->mgoi', c_mat, pw[:S5_CHUNK], b_bar, precision=hi).real
        lag = (ci[None, :] - ci[:, None]) if d == 0 else (ci[:, None] - ci[None, :])
        t_op = jnp.where((lag >= 0)[:, :, None, None, None], kern[jnp.clip(lag, 0, S5_CHUNK - 1)], 0.0)
        t_ops.append(jnp.transpose(t_op, (2, 0, 4, 1, 3)).reshape(S5_GROUPS, S5_ROW, S5_ROW))
        e_in = (S5_CHUNK - 1 - ci) if d == 0 else ci
        m_in = pw[e_in][:, :, :, None] * b_bar[None]
        m_in = jnp.transpose(m_in, (1, 0, 3, 2)).reshape(S5_GROUPS, S5_ROW, S5_STATE)
        in_ops.append(jnp.concatenate([m_in.real, m_in.imag], axis=-1))
        e_out = (ci + 1) if d == 0 else (S5_CHUNK - ci)
        m_out = c_mat[None] * pw[e_out][:, :, None, :]
        m_out = jnp.transpose(m_out, (1, 3, 0, 2)).reshape(S5_GROUPS, S5_STATE, S5_ROW)
        out_ops.append(jnp.concatenate([m_out.real, -m_out.imag], axis=1))
        a = pw[S5_CHUNK]
        mults = []
        for _ in range(n_steps):
            mults.append(a)
            a = a * a
        mults = jnp.stack(mults, axis=1)
        scan_ops.append(jnp.concatenate([jnp.concatenate([mults.real, mults.real], -1), jnp.concatenate([-mults.imag, mults.imag], -1)], axis=1))
    return (jnp.stack(t_ops).astype(BF16), jnp.stack(in_ops).astype(BF16), jnp.stack(out_ops).astype(BF16), jnp.stack(scan_ops))


def _s5_kernel(nb, n_steps, ul_ref, uc_ref, t_ref, in_ref, out_ref, sc_ref, yl_ref, yc_ref):
    n_l, n_c = ul_ref.shape[1], uc_ref.shape[1]

    def shift(x, rows, reverse):
        n = x.shape[0]
        row = lax.broadcasted_iota(jnp.int32, x.shape, 0)
        if reverse:
            return jnp.where(row < n - rows, pltpu.roll(x, n - rows, axis=0), 0.0)
        return jnp.where(row >= rows, pltpu.roll(x, rows, axis=0), 0.0)

    def group(g, carry):
        ul, uc = ul_ref[g], uc_ref[g]
        ys = []
        for d in range(2):
            rev = d == 1
            u = jnp.concatenate([ul, uc], axis=0) if rev else jnp.concatenate([uc, ul], axis=0)
            y = jnp.dot(u, t_ref[d, g], preferred_element_type=F32)
            x = jnp.dot(u, in_ref[d, g], preferred_element_type=F32)
            sc = sc_ref[d, g]
            for k in range(n_steps):
                if nb * 2 ** k >= x.shape[0]:
                    break
                xs = shift(x, nb * 2 ** k, rev)
                x = x + sc[k:k + 1] * xs + sc[n_steps + k:n_steps + k + 1] * pltpu.roll(xs, S5_STATE, axis=1)
            h_in = shift(x, nb, rev)
            ys.append(y + jnp.dot(h_in.astype(BF16), out_ref[d, g], preferred_element_type=F32))
        yl_ref[g] = ys[0][n_c:] + ys[1][:n_l]
        yc_ref[g] = ys[0][:n_c] + ys[1][n_l:]
        return carry
    lax.fori_loop(0, ul_ref.shape[0], group, 0)


def _s5_readout_kernel(y_ref, u_ref, d_ref, w_ref, o_ref):
    z = jax.nn.gelu(y_ref[...] + d_ref[...] * u_ref[...])
    o_ref[...] = (z * jax.nn.sigmoid(jnp.dot(z.astype(BF16), w_ref[...], preferred_element_type=F32))).astype(o_ref.dtype)


def _s5_mixer_pallas(proj, p, geo, rows):
    n_lat, seq, bsz, n_ctx = geo
    w = BRANCH_W
    nl, nc = seq // S5_CHUNK, (n_ctx // bsz) // S5_CHUNK
    n_steps = max(1, (nl + nc - 1).bit_length())
    t_op, in_op, out_op, scan_op = _s5_operators(p, nl + nc)

    def to_rows(u, n):
        u = u.reshape(bsz, n, S5_CHUNK, S5_GROUPS, S5_GROUP)
        return jnp.transpose(u, (3, 1, 0, 2, 4)).reshape(S5_GROUPS, n * bsz, S5_ROW)

    def from_rows(y, n):
        y = y.reshape(S5_GROUPS, n, bsz, S5_CHUNK, S5_GROUP)
        return jnp.transpose(y, (2, 1, 3, 0, 4)).reshape(bsz * n * S5_CHUNK, w)
    u = proj[:, :w]
    ul, uc = to_rows(u[:n_lat].astype(BF16), nl), to_rows(u[n_lat:].astype(BF16), nc)
    gb = S5_GBLK
    yl, yc = pl.pallas_call(
        functools.partial(_s5_kernel, bsz, n_steps),
        grid=(S5_GROUPS // gb,),
        in_specs=[pl.BlockSpec((gb, nl * bsz, S5_ROW), lambda i: (i, 0, 0)),
                  pl.BlockSpec((gb, nc * bsz, S5_ROW), lambda i: (i, 0, 0)),
                  pl.BlockSpec((2, gb, S5_ROW, S5_ROW), lambda i: (0, i, 0, 0)),
                  pl.BlockSpec((2, gb, S5_ROW, 2 * S5_STATE), lambda i: (0, i, 0, 0)),
                  pl.BlockSpec((2, gb, 2 * S5_STATE, S5_ROW), lambda i: (0, i, 0, 0)),
                  pl.BlockSpec((2, gb, 2 * n_steps, 2 * S5_STATE), lambda i: (0, i, 0, 0))],
        out_specs=[pl.BlockSpec((gb, nl * bsz, S5_ROW), lambda i: (i, 0, 0)),
                   pl.BlockSpec((gb, nc * bsz, S5_ROW), lambda i: (i, 0, 0))],
        out_shape=[jax.ShapeDtypeStruct((S5_GROUPS, nl * bsz, S5_ROW), F32),
                   jax.ShapeDtypeStruct((S5_GROUPS, nc * bsz, S5_ROW), F32)],
        compiler_params=_params(("parallel",)),
        name="s5_scan",
    )(ul, uc, t_op, in_op, out_op, scan_op)
    y = from_rows(yl, nl) if rows == n_lat else jnp.concatenate([from_rows(yl, nl), from_rows(yc, nc)], axis=0)
    tm = _row_tile(geo, 256)
    return pl.pallas_call(
        _s5_readout_kernel,
        grid=(rows // tm,),
        in_specs=[pl.BlockSpec((tm, w), lambda i: (i, 0)),
                  pl.BlockSpec((tm, w), lambda i: (i, 0)),
                  pl.BlockSpec((1, w), lambda i: (0, 0)),
                  pl.BlockSpec((w, w), lambda i: (0, 0))],
        out_specs=pl.BlockSpec((tm, w), lambda i: (i, 0)),
        out_shape=jax.ShapeDtypeStruct((rows, w), BF16),
        compiler_params=_params(("parallel",)),
        name="s5_readout",
    )(y, proj, p['s5_d'].reshape(1, w), p['s5_w_glu'].astype(BF16))


def _rope_tables(seq, blk):
    t = jnp.arange(seq)
    n_freq = DA_HEAD_DIM // 4
    inv = ROPE_BASE ** (-jnp.arange(n_freq, dtype=F32) / n_freq)
    ang = jnp.concatenate([(t // GRID_W).astype(F32)[:, None] * inv, (t % GRID_W).astype(F32)[:, None] * inv], axis=-1)
    cos = jnp.concatenate([jnp.cos(ang), jnp.cos(ang)], axis=-1)
    sin = jnp.concatenate([-jnp.sin(ang), jnp.sin(ang)], axis=-1)
    reps = BRANCH_W // DA_HEAD_DIM
    cos = jnp.concatenate([jnp.tile(cos, (1, reps)), jnp.ones((blk, BRANCH_W), F32)], axis=0)
    sin = jnp.concatenate([jnp.tile(sin, (1, reps)), jnp.zeros((blk, BRANCH_W), F32)], axis=0)
    return cos, sin


def _da_prep_kernel(q_ref, k_ref, v_ref, cos_ref, sin_ref, qo_ref, ko_ref, vo_ref):
    cos, sin = cos_ref[...], sin_ref[...]
    w = cos.shape[1]
    half = DA_HEAD_DIM // 2
    first = (lax.broadcasted_iota(jnp.int32, cos.shape, 1) % DA_HEAD_DIM) < half

    def rope(x):
        partner = jnp.where(first, pltpu.roll(x, w - half, axis=1), pltpu.roll(x, half, axis=1))
        return x * cos + partner * sin
    qo_ref[...] = (rope(q_ref[...]) * DA_HEAD_DIM ** -0.5).astype(qo_ref.dtype)
    ko_ref[...] = rope(k_ref[...]).astype(ko_ref.dtype)
    vo_ref[...] = v_ref[...].astype(vo_ref.dtype)


def _da_kernel(n_qb, q_ref, kl_ref, kc_ref, vl_ref, vc_ref, prm_ref, o_ref):
    lam = prm_ref[0:1, :]
    gain = prm_ref[1:2, :]
    nt = (((1,), (1,)), ((), ()))

    def attend(keys):
        q = q_ref[...]
        outs = []
        for m in range(2):
            lo, hi = m * DA_HEAD_DIM, (m + 1) * DA_HEAD_DIM
            s = [lax.dot_general(q[:, lo:hi], k_ref[:, lo:hi], nt, preferred_element_type=F32) for k_ref, _ in keys]
            mx = functools.reduce(jnp.maximum, [jnp.max(t, axis=-1, keepdims=True) for t in s])
            pr = [jnp.exp(t - mx) for t in s]
            den = functools.reduce(jnp.add, [jnp.sum(t, axis=-1, keepdims=True) for t in pr])
            num = functools.reduce(jnp.add, [jnp.dot(t.astype(BF16), v_ref[...], preferred_element_type=F32) for t, (_, v_ref) in zip(pr, keys)])
            outs.append(num / den)
        o = outs[0] - lam * outs[1]
        y = o * lax.rsqrt(jnp.mean(o * o, axis=-1, keepdims=True) + RMS_EPS) * gain
        o_ref[...] = y.astype(o_ref.dtype)

    @pl.when(pl.program_id(2) < n_qb)
    def _():
        attend([(kl_ref, vl_ref), (kc_ref, vc_ref)])

    @pl.when(pl.program_id(2) >= n_qb)
    def _():
        attend([(kc_ref, vc_ref)])


def _diff_attn_pallas(proj, p, lam_init, geo, rows):
    n_lat, seq, bsz, n_ctx = geo
    w = BRANCH_W
    n = proj.shape[0]
    tq = n_ctx // bsz
    n_qb = seq // tq
    cos, sin = _rope_tables(seq, tq)
    tab = lambda i: (jnp.where(i * tq < n_lat, i % n_qb, n_qb), 0)
    qr, kr, vb = pl.pallas_call(
        _da_prep_kernel,
        grid=(n // tq,),
        in_specs=[pl.BlockSpec((tq, w), lambda i: (i, 5)), pl.BlockSpec((tq, w), lambda i: (i, 6)), pl.BlockSpec((tq, w), lambda i: (i, 7)),
                  pl.BlockSpec((tq, w), tab), pl.BlockSpec((tq, w), tab)],
        out_specs=[pl.BlockSpec((tq, w), lambda i: (i, 0))] * 3,
        out_shape=[jax.ShapeDtypeStruct((n, w), BF16)] * 3,
        compiler_params=_params(("parallel",)),
        name="da_prep",
    )(proj, proj, proj, cos, sin)
    lam = jnp.exp(jnp.sum(p['da_lam_q1'] * p['da_lam_k1'])) - jnp.exp(jnp.sum(p['da_lam_q2'] * p['da_lam_k2'])) + lam_init
    prm = jnp.zeros((8, DA_V_DIM), F32).at[0].set(lam).at[1].set(p['da_subln'] * (1.0 - lam_init))
    with_ctx = rows > n_lat
    ctx_blk = n_lat // tq
    qrow = lambda b, h, i: (jnp.where(i < n_qb, b * n_qb + i, ctx_blk + b), h)
    return pl.pallas_call(
        functools.partial(_da_kernel, n_qb),
        grid=(bsz, DA_HEADS, n_qb + (1 if with_ctx else 0)),
        in_specs=[pl.BlockSpec((tq, DA_V_DIM), qrow),
                  pl.BlockSpec((seq, DA_V_DIM), lambda b, h, i: (b, h)),
                  pl.BlockSpec((tq, DA_V_DIM), lambda b, h, i: (ctx_blk + b, h)),
                  pl.BlockSpec((seq, DA_V_DIM), lambda b, h, i: (b, h)),
                  pl.BlockSpec((tq, DA_V_DIM), lambda b, h, i: (ctx_blk + b, h)),
                  pl.BlockSpec((8, DA_V_DIM), lambda b, h, i: (0, 0))],
        out_specs=pl.BlockSpec((tq, DA_V_DIM), qrow),
        out_shape=jax.ShapeDtypeStruct((rows, w), BF16),
        compiler_params=_params(("parallel", "parallel", "arbitrary")),
        name="diff_attn",
    )(qr, kr, kr, vb, vb, prm)


LRU_CBLK = 256
SUBLANES = 8


def _lru_kernel(xl_ref, xc_ref, gl_ref, gc_ref, cw_ref, cb_ref, wr_ref, br_ref, wi_ref, bi_ref, lam_ref, ol_ref, oc_ref, a_sc, b_sc, h_sc):
    n_l, n_c = xl_ref.shape[0], xc_ref.shape[0]
    n_all = n_l + n_c
    cw, cb = cw_ref[...], cb_ref[...]

    def conv(x):
        n = x.shape[0]
        row = lax.broadcasted_iota(jnp.int32, x.shape, 0)
        xm2 = jnp.where(row >= 2, pltpu.roll(x, 2, axis=0), 0.0)
        xm1 = jnp.where(row >= 1, pltpu.roll(x, 1, axis=0), 0.0)
        xp1 = jnp.where(row < n - 1, pltpu.roll(x, n - 1, axis=0), 0.0)
        return cw[0:1] * xm2 + cw[1:2] * xm1 + cw[2:3] * x + cw[3:4] * xp1 + cb
    xc = jnp.concatenate([conv(xc_ref[...]), conv(xl_ref[...])], axis=0)
    xb = xc.astype(BF16)
    sub = lax.broadcasted_iota(jnp.int32, xc.shape, 0) % SUBLANES
    n_blk = xc.shape[1] // LRU_BLOCK_W

    def gate(w_ref, b_ref, d):
        cols = [jnp.dot(xb[:, k * LRU_BLOCK_W:(k + 1) * LRU_BLOCK_W], w_ref[d, k], preferred_element_type=F32) for k in range(n_blk)]
        return jax.nn.sigmoid(jnp.concatenate(cols, axis=1) + b_ref[d])

    for d in range(2):
        rev = d == 1
        r, i = gate(wr_ref, br_ref, d), gate(wi_ref, bi_ref, d)
        log_a = -LRU_C * r * jax.nn.softplus(-lam_ref[d])
        a = jnp.exp(log_a)
        b = jnp.sqrt(1.0 - a * a) * (i * xc)
        for s in (1, 2, 4):
            if rev:
                keep = sub < SUBLANES - s
                a_n = jnp.where(keep, pltpu.roll(a, n_all - s, axis=0), 1.0)
                b_n = jnp.where(keep, pltpu.roll(b, n_all - s, axis=0), 0.0)
            else:
                keep = sub >= s
                a_n = jnp.where(keep, pltpu.roll(a, s, axis=0), 1.0)
                b_n = jnp.where(keep, pltpu.roll(b, s, axis=0), 0.0)
            b = b + a * b_n
            a = a * a_n
        a_sc[...] = a
        b_sc[...] = b
        last = 0 if rev else SUBLANES - 1

        def tile_step(t, carry):
            rows = pl.ds(pl.multiple_of(t * SUBLANES, SUBLANES), SUBLANES)
            h = b_sc[rows, :] + a_sc[rows, :] * carry
            if rev:
                h_sc[rows, :] += h
            else:
                h_sc[rows, :] = h
            return jnp.broadcast_to(h[last:last + 1], h.shape)
        carry = jnp.zeros((SUBLANES, xc.shape[1]), F32)
        tc, ta = n_c // SUBLANES, n_all // SUBLANES
        if rev:
            carry = lax.fori_loop(0, tc, lambda t, c: tile_step(tc - 1 - t, c), carry)
            lax.fori_loop(0, ta - tc, lambda t, c: tile_step(ta - 1 - t, c), carry)
        else:
            lax.fori_loop(0, ta, tile_step, carry)
    oc_ref[...] = (h_sc[0:n_c, :] * jax.nn.gelu(gc_ref[...])).astype(oc_ref.dtype)
    ol_ref[...] = (h_sc[n_c:n_all, :] * jax.nn.gelu(gl_ref[...])).astype(ol_ref.dtype)


def _rglru_pallas(proj, p, geo):
    n_lat, seq, bsz, n_ctx = geo
    w = BRANCH_W
    ctx_len = n_ctx // bsz
    cblk = LRU_CBLK
    ncb = w // cblk
    gpb = cblk // LRU_BLOCK_W
    lat_rows, ctx_rows = seq // ctx_len, n_lat // ctx_len
    xcol, gcol = 8 * ncb, 9 * ncb
    vec = lambda t: t.reshape(2, 1, w)
    return pl.pallas_call(
        _lru_kernel,
        grid=(bsz, ncb),
        in_specs=[pl.BlockSpec((seq, cblk), lambda b, c: (b, xcol + c)),
                  pl.BlockSpec((ctx_len, cblk), lambda b, c: (ctx_rows + b, xcol + c)),
                  pl.BlockSpec((seq, cblk), lambda b, c: (b, gcol + c)),
                  pl.BlockSpec((ctx_len, cblk), lambda b, c: (ctx_rows + b, gcol + c)),
                  pl.BlockSpec((4, cblk), lambda b, c: (0, c)),
                  pl.BlockSpec((1, cblk), lambda b, c: (0, c)),
                  pl.BlockSpec((2, gpb, LRU_BLOCK_W, LRU_BLOCK_W), lambda b, c: (0, c, 0, 0)),
                  pl.BlockSpec((2, 1, cblk), lambda b, c: (0, 0, c)),
                  pl.BlockSpec((2, gpb, LRU_BLOCK_W, LRU_BLOCK_W), lambda b, c: (0, c, 0, 0)),
                  pl.BlockSpec((2, 1, cblk), lambda b, c: (0, 0, c)),
                  pl.BlockSpec((2, 1, cblk), lambda b, c: (0, 0, c))],
        out_specs=[pl.BlockSpec((seq, cblk), lambda b, c: (b, c)),
                   pl.BlockSpec((ctx_len, cblk), lambda b, c: (b, c))],
        out_shape=[jax.ShapeDtypeStruct((n_lat, w), BF16), jax.ShapeDtypeStruct((n_ctx, w), BF16)],
        scratch_shapes=[pltpu.VMEM((seq + ctx_len, cblk), F32)] * 3,
        compiler_params=_params(("parallel", "parallel")),
        name="rglru",
    )(proj, proj, proj, proj, p['lru_conv_w'], p['lru_conv_b'].reshape(1, w), p['lru_w_rg'].astype(BF16), vec(p['lru_b_rg']),
      p['lru_w_ig'].astype(BF16), vec(p['lru_b_ig']), vec(p['lru_lam']))


DN_PREP_CBLK = 512
DN_HPB = 2
DN_DOUBLINGS = 5


def _dn_prep_kernel(xl_ref, xc_ref, cw_ref, ol_ref, oc_ref):
    cw = cw_ref[...]
    cblk = cw.shape[1]
    cb = pl.program_id(1)
    n_q = BRANCH_W // cblk

    def prep(x):
        n = x.shape[0]
        row = lax.broadcasted_iota(jnp.int32, x.shape, 0)
        xm2 = jnp.where(row >= 2, pltpu.roll(x, 2, axis=0), 0.0)
        xm1 = jnp.where(row >= 1, pltpu.roll(x, 1, axis=0), 0.0)
        xp1 = jnp.where(row < n - 1, pltpu.roll(x, n - 1, axis=0), 0.0)
        y = cw[0:1] * xm2 + cw[1:2] * xm1 + cw[2:3] * x + cw[3:4] * xp1
        y = y * jax.nn.sigmoid(y)
        heads = [y[:, h * DN_HEAD_DIM:(h + 1) * DN_HEAD_DIM] for h in range(cblk // DN_HEAD_DIM)]
        normed = jnp.concatenate([t * lax.rsqrt(jnp.sum(t * t, axis=-1, keepdims=True) + 1e-6) for t in heads], axis=1)
        scale = jnp.where(cb < n_q, DN_HEAD_DIM ** -0.5, 1.0)
        return jnp.where(cb < 2 * n_q, normed * scale, y)
    ol_ref[...] = prep(xl_ref[...]).astype(ol_ref.dtype)
    oc_ref[...] = prep(xc_ref[...]).astype(oc_ref.dtype)


def _dn_kernel(ql_ref, qc_ref, kl_ref, kc_ref, vl_ref, vc_ref, zl_ref, zc_ref, abl_ref, abc_ref, prm_ref, ol_ref, oc_ref,
               q_sc, k_sc, v_sc, g_sc, b_sc, o_sc):
    n_l, n_c = ql_ref.shape[0], qc_ref.shape[0]
    n_all = n_l + n_c
    c = DN_CHUNK
    hd = DN_HEAD_DIM
    ncc, nch = n_c // c, n_all // c
    hp = pl.program_id(1)
    for sc, rc, rl in ((q_sc, qc_ref, ql_ref), (k_sc, kc_ref, kl_ref), (v_sc, vc_ref, vl_ref)):
        sc[0:n_c, :] = rc[...]
        sc[n_c:n_all, :] = rl[...]
    ab = jnp.concatenate([abc_ref[...], abl_ref[...]], axis=0)
    g_all = prm_ref[0:1, :] * jax.nn.softplus(ab + prm_ref[1:2, :])
    beta_all = jax.nn.sigmoid(ab)
    lane = lax.broadcasted_iota(jnp.int32, ab.shape, 1)
    pos = lax.broadcasted_iota(jnp.int32, ab.shape, 0) % c
    for d in range(2):
        for s in range(DN_HPB):
            col = d * DN_HEADS + hp * DN_HPB + s
            g = jnp.broadcast_to(jnp.sum(jnp.where(lane == col, g_all, 0.0), axis=-1, keepdims=True), ab.shape)
            step = 1
            while step < c:
                if d == 0:
                    g = g + jnp.where(pos >= step, pltpu.roll(g, step, axis=0), 0.0)
                else:
                    g = g + jnp.where(pos < c - step, pltpu.roll(g, n_all - step, axis=0), 0.0)
                step *= 2
            g_sc[d, s] = g
            b_sc[d, s] = jnp.broadcast_to(jnp.sum(jnp.where(lane == 2 * DN_HEADS + col, beta_all, 0.0), axis=-1, keepdims=True), ab.shape)
    nch4 = 2 * DN_HPB
    cw = nch4 * c
    r_w = lax.broadcasted_iota(jnp.int32, (c, cw), 0)
    l_w = lax.broadcasted_iota(jnp.int32, (c, cw), 1)
    j_w, blk_w = l_w % c, l_w // c
    ahead_w = jnp.where((blk_w % 2) == 1, j_w - r_w, r_w - j_w)
    incl_w = ahead_w >= 0
    strict_w = ahead_w > 0
    eye_w = (r_w == j_w).astype(F32)
    bd_mask = (lax.broadcasted_iota(jnp.int32, (cw, cw), 0) // c) == (lax.broadcasted_iota(jnp.int32, (cw, cw), 1) // c)
    blk_t = lax.broadcasted_iota(jnp.int32, (hd, cw), 1) // c
    nt = (((1,), (1,)), ((), ()))

    def block_diag(wide):
        return jnp.where(bd_mask, jnp.concatenate([wide] * nch4, axis=0), 0.0)

    def pick(blocks):
        out = blocks[-1]
        for i in range(nch4 - 2, -1, -1):
            out = jnp.where(blk_w == i, blocks[i], out)
        return out

    def split(m):
        hi = m.astype(BF16)
        return hi, (m - hi.astype(F32)).astype(BF16)

    def mm_split(lhs, rhs):
        n = lhs.shape[0]
        lh, ll = split(lhs)
        rh, rl = split(rhs)
        top = jnp.dot(jnp.concatenate([lh, ll], axis=0), rh, preferred_element_type=F32)
        return top[:n] + top[n:] + jnp.dot(lh, rl, preferred_element_type=F32)

    def body(t, states):
        n_rev = jnp.where(t < ncc, ncc - 1 - t, nch - 1 - (t - ncc))
        chains = [divmod(i, 2) for i in range(nch4)]
        rows = [pl.ds(pl.multiple_of((t if d == 0 else n_rev) * c, c), c) for _, d in chains]
        hs = [slice(s * hd, (s + 1) * hd) for s, _ in chains]
        q = [q_sc[rows[i], hs[i]] for i in range(nch4)]
        k = [k_sc[rows[i], hs[i]] for i in range(nch4)]
        gc = [g_sc[d, s, rows[i], :] for i, (s, d) in enumerate(chains)]
        beta = [b_sc[d, s, rows[i], :] for i, (s, d) in enumerate(chains)]
        kf = [t_.astype(F32) for t_ in k]
        kb = [kf[i] * beta[i] for i in range(nch4)]
        lhs = jnp.concatenate([jnp.concatenate([kb[i].astype(BF16), q[i]], axis=0) for i in range(nch4)], axis=0)
        r = lax.dot_general(lhs, jnp.concatenate(k, axis=0), nt, preferred_element_type=F32)
        a_raw = pick([r[2 * c * i:2 * c * i + c] for i in range(nch4)])
        qk_raw = pick([r[2 * c * i + c:2 * c * (i + 1)] for i in range(nch4)])
        g_col = pick([jnp.concatenate([gc[i]] * (cw // LANES), axis=1) for i in range(nch4)])
        g_row = jnp.concatenate(gc, axis=0).T[:c, :]
        decay = jnp.exp(jnp.where(incl_w, g_col - g_row, -jnp.inf))
        a = jnp.where(strict_w, a_raw, 0.0) * decay
        qk = qk_raw * decay
        x = eye_w - a
        pw = mm_split(a, block_diag(a))
        for lvl in range(1, DN_DOUBLINGS + 1):
            bd = block_diag(pw)
            if lvl < DN_DOUBLINGS:
                both = mm_split(jnp.concatenate([pw, x], axis=0), bd)
                pw, x = both[:c], x + both[c:]
            else:
                x = x + mm_split(x, bd)
        eg = [jnp.exp(t_) for t_ in gc]
        rhs = jnp.concatenate([jnp.concatenate([v_sc[rows[i], hs[i]].astype(F32) * beta[i], kb[i] * eg[i]], axis=1)
                               for i in range(nch4)], axis=0).astype(BF16)
        uw = jnp.dot(block_diag(x).astype(BF16), rhs, preferred_element_type=F32)
        zero = jnp.zeros((2 * c, hd), BF16)
        w_s, q_s = [], []
        for i0 in range(0, nch4, 2):
            parts = [jnp.concatenate([uw[c * i:c * (i + 1), hd:], q[i].astype(F32) * eg[i]], axis=0).astype(BF16) for i in (i0, i0 + 1)]
            lhs2 = jnp.concatenate([jnp.concatenate([parts[0], zero], axis=1), jnp.concatenate([zero, parts[1]], axis=1)], axis=0)
            sb = jnp.concatenate([states[i0], states[i0 + 1]], axis=0).astype(BF16)
            pr = jnp.dot(lhs2, sb, preferred_element_type=F32)
            for j in range(2):
                w_s.append(pr[2 * c * j:2 * c * j + c])
                q_s.append(pr[2 * c * j + c:2 * c * (j + 1)])
        v_new = jnp.concatenate([uw[c * i:c * (i + 1), :hd] - w_s[i] for i in range(nch4)], axis=0).astype(BF16)
        ov = jnp.dot(block_diag(qk).astype(BF16), v_new, preferred_element_type=F32)
        for i, (s, d) in enumerate(chains):
            o_sc[d, rows[i], hs[i]] = q_s[i] + ov[c * i:c * (i + 1)]
        g_last = [gc[i][c - 1:c] if d == 0 else gc[i][0:1] for i, (_, d) in enumerate(chains)]
        k_dec_t = jnp.concatenate([kf[i] * jnp.exp(g_last[i] - gc[i]) for i in range(nch4)], axis=0).T
        lhs3 = jnp.concatenate([jnp.where(blk_t == i, k_dec_t, 0.0) for i in range(nch4)], axis=0).astype(BF16)
        upd = jnp.dot(lhs3, v_new, preferred_element_type=F32)
        return tuple(states[i] * jnp.exp(g_last[i]) + upd[hd * i:hd * (i + 1)] for i in range(nch4))
    lax.fori_loop(0, nch, body, tuple(jnp.zeros((hd, hd), F32) for _ in range(nch4)))

    gain = prm_ref[2:3, :]

    def readout(o, z):
        cols = []
        for s in range(DN_HPB):
            t, zz = o[:, s * hd:(s + 1) * hd], z[:, s * hd:(s + 1) * hd]
            cols.append(t * lax.rsqrt(jnp.mean(t * t, axis=-1, keepdims=True) + RMS_EPS) * gain * (zz * jax.nn.sigmoid(zz)))
        return jnp.concatenate(cols, axis=1)
    oc_ref[...] = readout(o_sc[0, 0:n_c, :] + o_sc[1, 0:n_c, :], zc_ref[...]).astype(oc_ref.dtype)
    ol_ref[...] = readout(o_sc[0, n_c:n_all, :] + o_sc[1, n_c:n_all, :], zl_ref[...]).astype(ol_ref.dtype)


def _deltanet_pallas(proj, p, geo):
    n_lat, seq, bsz, n_ctx = geo
    w = BRANCH_W
    ctx_len = n_ctx // bsz
    ctx_rows = n_lat // ctx_len
    cblk = DN_PREP_CBLK
    c0 = w // cblk
    qkv_l, qkv_c = pl.pallas_call(
        _dn_prep_kernel,
        grid=(bsz, 3 * w // cblk),
        in_specs=[pl.BlockSpec((seq, cblk), lambda b, j: (b, c0 + j)),
                  pl.BlockSpec((ctx_len, cblk), lambda b, j: (ctx_rows + b, c0 + j)),
                  pl.BlockSpec((4, cblk), lambda b, j: (0, j))],
        out_specs=[pl.BlockSpec((seq, cblk), lambda b, j: (b, j)),
                   pl.BlockSpec((ctx_len, cblk), lambda b, j: (b, j))],
        out_shape=[jax.ShapeDtypeStruct((n_lat, 3 * w), BF16), jax.ShapeDtypeStruct((n_ctx, 3 * w), BF16)],
        compiler_params=_params(("parallel", "parallel")),
        name="dn_prep",
    )(proj, proj, p['dn_conv_w'])
    hw = DN_HPB * DN_HEAD_DIM
    nhb = w // hw
    zcol, abcol = 4 * w // hw, 10 * w // LANES
    n16 = 2 * DN_HEADS
    prm = (jnp.zeros((8, LANES), F32).at[0, :n16].set(-jnp.exp(p['dn_a_log']).reshape(n16))
           .at[1, :n16].set(p['dn_dt_bias'].reshape(n16)).at[2, :DN_HEAD_DIM].set(p['dn_norm']))
    lat = lambda off: pl.BlockSpec((seq, hw), lambda b, h: (b, off + h))
    ctxs = lambda off: pl.BlockSpec((ctx_len, hw), lambda b, h: (b, off + h))
    n_all = seq + ctx_len
    return pl.pallas_call(
        _dn_kernel,
        grid=(bsz, nhb),
        in_specs=[lat(0), ctxs(0), lat(nhb), ctxs(nhb), lat(2 * nhb), ctxs(2 * nhb),
                  pl.BlockSpec((seq, hw), lambda b, h: (b, zcol + h)),
                  pl.BlockSpec((ctx_len, hw), lambda b, h: (ctx_rows + b, zcol + h)),
                  pl.BlockSpec((seq, LANES), lambda b, h: (b, abcol)),
                  pl.BlockSpec((ctx_len, LANES), lambda b, h: (ctx_rows + b, abcol)),
                  pl.BlockSpec((8, LANES), lambda b, h: (0, 0))],
        out_specs=[pl.BlockSpec((seq, hw), lambda b, h: (b, h)),
                   pl.BlockSpec((ctx_len, hw), lambda b, h: (b, h))],
        out_shape=[jax.ShapeDtypeStruct((n_lat, w), BF16), jax.ShapeDtypeStruct((n_ctx, w), BF16)],
        scratch_shapes=[pltpu.VMEM((n_all, hw), BF16)] * 3
        + [pltpu.VMEM((2, DN_HPB, n_all, LANES), F32)] * 2
        + [pltpu.VMEM((2, n_all, hw), F32)],
        compiler_params=_params(("parallel", "parallel")),
        name="deltanet",
    )(qkv_l, qkv_c, qkv_l, qkv_c, qkv_l, qkv_c, proj, proj, proj, proj, prm)


def _mixers(proj, p, lam_init, geo, rows):
    n_lat = geo[0]
    with_ctx = rows > n_lat
    dn_l, dn_c = _deltanet_pallas(proj, p, geo)
    lru_l, lru_c = _rglru_pallas(proj, p, geo)

    def join(lat, ctx_part):
        return jnp.concatenate([lat, ctx_part], axis=0) if with_ctx else lat
    return [_s5_mixer_pallas(proj, p, geo, rows), join(dn_l, dn_c), _diff_attn_pallas(proj, p, lam_init, geo, rows), join(lru_l, lru_c)]


def kernel(x, c, ctx, c_ctx, ada_w, ada_b, mix_norm, ffn_norm, w_in, s5_lam_re, s5_lam_im, s5_log_dt, s5_b_re, s5_b_im, s5_c_re, s5_c_im, s5_d, s5_w_glu, dn_conv_w, dn_a_log, dn_dt_bias, dn_norm, da_lam_q1, da_lam_k1, da_lam_q2, da_lam_k2, da_subln, lru_conv_w, lru_conv_b, lru_w_rg, lru_b_rg, lru_w_ig, lru_b_ig, lru_lam, w_branch, w_out, router_w, router_b, moe_w_gu, moe_b_gu, moe_w_down, moe_b_down, final_norm):
    bsz, seq, d = x.shape
    n_ctx_tok = ctx.shape[1]
    depth = ada_w.shape[0]
    n_ctx = bsz * n_ctx_tok
    n_lat = bsz * seq
    n_tok = n_ctx + n_lat
    geo = (n_lat, seq, bsz, n_ctx)
    w = BRANCH_W
    mix_cols = 10 * w + 4 * DN_HEADS
    ab_lo = 5 * w

    xs = jnp.concatenate([x.reshape(n_lat, d), ctx.reshape(n_ctx, d)], axis=0)
    mods = _ada_table(c, c_ctx, ada_w, ada_b)

    for l in range(depth):
        with_ctx = l < depth - 1
        lam_init = 0.8 - 0.6 * math.exp(-0.3 * l)
        rows = n_tok if with_ctx else n_lat
        mod = mods[l]
        p = {'s5_lam_re': s5_lam_re[l], 's5_lam_im': s5_lam_im[l], 's5_log_dt': s5_log_dt[l], 's5_b_re': s5_b_re[l], 's5_b_im': s5_b_im[l], 's5_c_re': s5_c_re[l], 's5_c_im': s5_c_im[l], 's5_d': s5_d[l], 's5_w_glu': s5_w_glu[l], 'dn_conv_w': dn_conv_w[l], 'dn_a_log': dn_a_log[l], 'dn_dt_bias': dn_dt_bias[l], 'dn_norm': dn_norm[l], 'da_lam_q1': da_lam_q1[l], 'da_lam_k1': da_lam_k1[l], 'da_lam_q2': da_lam_q2[l], 'da_lam_k2': da_lam_k2[l], 'da_subln': da_subln[l], 'lru_conv_w': lru_conv_w[l], 'lru_conv_b': lru_conv_b[l], 'lru_w_rg': lru_w_rg[l], 'lru_b_rg': lru_b_rg[l], 'lru_w_ig': lru_w_ig[l], 'lru_b_ig': lru_b_ig[l], 'lru_lam': lru_lam[l]}
        wl = w_in[l]
        w_mix = jnp.concatenate([wl[:, :ab_lo], wl[:, ab_lo + 4 * DN_HEADS:mix_cols], wl[:, ab_lo:ab_lo + 4 * DN_HEADS],
                                 jnp.zeros((d, AB_COLS - 4 * DN_HEADS), F32)], axis=1).astype(BF16)
        w_gate = wl[:, mix_cols:].astype(BF16)
        wgu = moe_w_gu[l].astype(BF16)
        bgu = moe_b_gu[l][:, None, :]
        wd = moe_w_down[l].astype(BF16)
        wd = jnp.stack([wd, jnp.zeros_like(wd)], axis=2).reshape(wd.shape[0], 2 * wd.shape[1], d)
        bd = moe_b_down[l][:, None, :]

        h = _norm_mod(xs, mix_norm[l], mod, geo)
        proj = _matmul(h, w_mix, F32)
        branches = _mixers(proj, p, lam_init, geo, rows)
        merged = _merge(h, branches, w_gate, w_branch[l].astype(BF16), rows)
        xs = _outproj_residual(merged, w_out[l].astype(BF16), xs, mod, geo, rows)
        final = l == depth - 1
        xs = _moe_layer(xs, ffn_norm[l], mod, router_w[l], router_b[l], wgu, bgu, wd, bd, final_norm, geo, rows, final)
    return xs.reshape(bsz, seq, d)
```
